```python
import jax, jax.numpy as jnp
from jax import lax
import numpy as np

D_MODEL = 1024
BATCH = 4
SEQ = 4096
DEPTH = 2

MEM_LEN = 256
N_EVEN = (DEPTH + 1) // 2
N_ODD = DEPTH // 2

MIX_WIDTH = D_MODEL
A_WIDTH = MIX_WIDTH // 2
A_CONV = 3
B_WIDTH = MIX_WIDTH - A_WIDTH
POOL_WINDOWS = (2, 4, 8, 16)
B_GROUP = B_WIDTH // len(POOL_WINDOWS)
IN_COLS = 3 * A_WIDTH + B_WIDTH

C_CONV = 31

XATTN_HEADS = 4
XATTN_HEAD_DIM = D_MODEL // XATTN_HEADS

D_FF = ((8 * D_MODEL // 3 + 255) // 256) * 256
N_EXPERTS = 8
TOP_K = 2
D_FF_EXPERT = 7 * D_MODEL // 2

EPS = 1e-6

kernel_name = "hybrid_conv_pool_conformer_moe_encoder"


def rmsnorm(x, g):
    xf = x.astype(jnp.float32)
    y = xf * lax.rsqrt(jnp.mean(xf * xf, axis=-1, keepdims=True) + EPS)
    return (y * g.astype(jnp.float32)).astype(x.dtype)


def layernorm(x, g, b):
    xf = x.astype(jnp.float32)
    mu = jnp.mean(xf, axis=-1, keepdims=True)
    xc = xf - mu
    var = jnp.mean(xc * xc, axis=-1, keepdims=True)
    y = xc * lax.rsqrt(var + EPS) * g.astype(jnp.float32) + b.astype(jnp.float32)
    return y.astype(x.dtype)


def depthwise_conv(x, w):
    k = w.shape[0]
    c = x.shape[-1]
    return lax.conv_general_dilated(
        x, w[:, None, :].astype(x.dtype), window_strides=(1,),
        padding=[(k // 2, k // 2)], dimension_numbers=("NWC", "WIO", "NWC"),
        feature_group_count=c)


def centred_window_mean(u, window):
    s = u.shape[1]
    left = window // 2
    right = window - 1 - left
    t = np.arange(s)
    lo = np.maximum(t - left, 0)
    hi = np.minimum(t + right, s - 1) + 1
    uf = u.astype(jnp.float32)
    cs = jnp.concatenate([jnp.zeros_like(uf[:, :1]), jnp.cumsum(uf, axis=1)], axis=1)
    count = jnp.asarray((hi - lo).astype(np.float32))[None, :, None]
    return ((cs[:, hi] - cs[:, lo]) / count).astype(u.dtype)


def short_conv_mixer(h, gate_b, gate_c, conv_w):
    return gate_b * depthwise_conv(gate_c * h, conv_w)


def pool_mixer(u, w_groups, scale):
    outs = []
    for g, win in enumerate(POOL_WINDOWS):
        ug = u[..., g * B_GROUP:(g + 1) * B_GROUP]
        outs.append(centred_window_mean(ug, win) - ug)
    p = jnp.stack(outs, axis=2)
    y = jnp.einsum("bsgc,gcd->bsgd", p, w_groups)
    return y.reshape(u.shape) * scale


def even_mixer(xn, w_in, conv_a, pool_w, pool_scale, w_out):
    z = xn @ w_in
    h, gb, gc, u = jnp.split(z, [A_WIDTH, 2 * A_WIDTH, 3 * A_WIDTH], axis=-1)
    y = jnp.concatenate([short_conv_mixer(h, gb, gc, conv_a),
                         pool_mixer(u, pool_w, pool_scale)], axis=-1)
    return y @ w_out


def conformer_conv(xn, pw1_w, pw1_b, dw_w, dw_b, ln_g, ln_b, pw2_w, pw2_b):
    a, g = jnp.split(xn @ pw1_w + pw1_b, 2, axis=-1)
    h = a * jax.nn.sigmoid(g)
    h = depthwise_conv(h, dw_w) + dw_b
    h = jax.nn.silu(layernorm(h, ln_g, ln_b))
    return h @ pw2_w + pw2_b


def cross_attn(xn, mem, norm_mem, wq, wkv, wo):
    b, s, d = xn.shape
    m = mem.shape[1]
    q = (xn @ wq).reshape(b, s, XATTN_HEADS, XATTN_HEAD_DIM)
    k, v = jnp.split(rmsnorm(mem, norm_mem) @ wkv, 2, axis=-1)
    k = k.reshape(b, m, XATTN_HEADS, XATTN_HEAD_DIM)
    v = v.reshape(b, m, XATTN_HEADS, XATTN_HEAD_DIM)
    sc = jnp.einsum("bshd,bmhd->bhsm", q, k).astype(jnp.float32) * (XATTN_HEAD_DIM ** -0.5)
    p = jax.nn.softmax(sc, axis=-1).astype(v.dtype)
    o = jnp.einsum("bhsm,bmhd->bshd", p, v).reshape(b, s, d)
    return o @ wo


def swiglu(xn, w_gu, w_down):
    g, u = jnp.split(xn @ w_gu, 2, axis=-1)
    return (jax.nn.silu(g) * u) @ w_down


def moe_swiglu(xn, router, w_gu, w_down):
    logits = (xn @ router).astype(jnp.float32)
    top_v, top_i = lax.top_k(logits, TOP_K)
    w = jax.nn.softmax(top_v, axis=-1)
    comb = jnp.sum(jax.nn.one_hot(top_i, N_EXPERTS, dtype=jnp.float32) * w[..., None],
                   axis=-2).astype(xn.dtype)
    out = jnp.zeros_like(xn)
    for e in range(N_EXPERTS):
        out = out + comb[..., e:e + 1] * swiglu(xn, w_gu[e], w_down[e])
    return out


def setup_inputs(seed: int = 0) -> dict:
    key = jax.random.key(seed)
    ks = iter(jax.random.split(key, 40))
    f32 = jnp.float32

    def nrm(shape, scale):
        return jax.random.normal(next(ks), shape, f32) * scale

    def gain(shape):
        return 1.0 + 0.02 * jax.random.normal(next(ks), shape, f32)

    D = D_MODEL
    return {
        "x": nrm((BATCH, SEQ, D), 1.0),
        "mem": nrm((BATCH, MEM_LEN, D), 1.0),
        "ev_norm_mix": gain((N_EVEN, D)),
        "ev_w_in": nrm((N_EVEN, D, IN_COLS), D ** -0.5),
        "ev_conv_a": nrm((N_EVEN, A_CONV, A_WIDTH), A_CONV ** -0.5),
        "ev_pool_w": nrm((N_EVEN, len(POOL_WINDOWS), B_GROUP, B_GROUP), B_GROUP ** -0.5),
        "ev_pool_scale": gain((N_EVEN, B_WIDTH)),
        "ev_w_out": nrm((N_EVEN, MIX_WIDTH, D), MIX_WIDTH ** -0.5),
        "ev_norm_ffn": gain((N_EVEN, D)),
        "ev_ffn_gu": nrm((N_EVEN, D, 2 * D_FF), D ** -0.5),
        "ev_ffn_down": nrm((N_EVEN, D_FF, D), D_FF ** -0.5),
        "od_norm_mix": gain((N_ODD, D)),
        "od_pw1_w": nrm((N_ODD, D, 2 * D), D ** -0.5),
        "od_pw1_b": nrm((N_ODD, 2 * D), 0.02),
        "od_dw_w": nrm((N_ODD, C_CONV, D), C_CONV ** -0.5),
        "od_dw_b": nrm((N_ODD, D), 0.02),
        "od_ln_g": gain((N_ODD, D)),
        "od_ln_b": nrm((N_ODD, D), 0.02),
        "od_pw2_w": nrm((N_ODD, D, D), D ** -0.5),
        "od_pw2_b": nrm((N_ODD, D), 0.02),
        "od_norm_moe": gain((N_ODD, D)),
        "od_router": nrm((N_ODD, D, N_EXPERTS), D ** -0.5),
        "od_moe_gu": nrm((N_ODD, N_EXPERTS, D, 2 * D_FF_EXPERT), D ** -0.5),
        "od_moe_down": nrm((N_ODD, N_EXPERTS, D_FF_EXPERT, D), D_FF_EXPERT ** -0.5),
        "xa_norm": gain((DEPTH, D)),
        "xa_norm_mem": gain((DEPTH, D)),
        "xa_wq": nrm((DEPTH, D, D), D ** -0.5),
        "xa_wkv": nrm((DEPTH, D, 2 * D), D ** -0.5),
        "xa_wo": nrm((DEPTH, D, D), D ** -0.5),
        "final_norm": gain((D,)),
    }


def reference(x, mem,
              ev_norm_mix, ev_w_in, ev_conv_a, ev_pool_w, ev_pool_scale, ev_w_out,
              ev_norm_ffn, ev_ffn_gu, ev_ffn_down,
              od_norm_mix, od_pw1_w, od_pw1_b, od_dw_w, od_dw_b, od_ln_g, od_ln_b,
              od_pw2_w, od_pw2_b, od_norm_moe, od_router, od_moe_gu, od_moe_down,
              xa_norm, xa_norm_mem, xa_wq, xa_wkv, xa_wo, final_norm):
    h = x
    for layer in range(DEPTH):
        i = layer // 2
        if layer % 2 == 0:
            h = h + even_mixer(rmsnorm(h, ev_norm_mix[i]), ev_w_in[i], ev_conv_a[i],
                               ev_pool_w[i], ev_pool_scale[i], ev_w_out[i])
        else:
            h = h + conformer_conv(rmsnorm(h, od_norm_mix[i]), od_pw1_w[i], od_pw1_b[i],
                                   od_dw_w[i], od_dw_b[i], od_ln_g[i], od_ln_b[i],
                                   od_pw2_w[i], od_pw2_b[i])
        h = h + cross_attn(rmsnorm(h, xa_norm[layer]), mem, xa_norm_mem[layer],
                           xa_wq[layer], xa_wkv[layer], xa_wo[layer])
        if layer % 2 == 0:
            h = h + swiglu(rmsnorm(h, ev_norm_ffn[i]), ev_ffn_gu[i], ev_ffn_down[i])
        else:
            h = h + moe_swiglu(rmsnorm(h, od_norm_moe[i]), od_router[i],
                               od_moe_gu[i], od_moe_down[i])
    return rmsnorm(h, final_norm)
```

```python
import functools

import jax
import jax.numpy as jnp
from jax import lax
from jax.experimental import pallas as pl
from jax.experimental.pallas import tpu as pltpu

EPS = 1e-6
BF16 = jnp.bfloat16
F32 = jnp.float32

POOL_WINDOWS = (2, 4, 8, 16)
XATTN_HEADS = 4
N_EXPERTS = 8
LANES = 128
SUBLANES = 8
VMEM_LIMIT = 56 * 1024 * 1024

SEQ_TILE = 512
MIX_HALO = 8
CONF_HALO = 16
CONV_ROWS = 32
FFN_CHUNK = 256
MOE_ROWS = 512
MOE_CHUNK = 512
TOK_CHUNK = 256


def _rms(x, g):
    return x * lax.rsqrt(jnp.mean(x * x, axis=-1, keepdims=True) + EPS) * g


def _dot(a, b):
    return jnp.dot(a, b, preferred_element_type=F32)


def _cparams(*sem):
    return pltpu.CompilerParams(dimension_semantics=sem, vmem_limit_bytes=VMEM_LIMIT)


def _const_spec(shape):
    nd = len(shape)
    return pl.BlockSpec(shape, lambda *_: (0,) * nd, pipeline_mode=pl.Buffered(1))


def _even_mixer_kernel(x_ref, xp_ref, xn_ref, g_ref, win_ref, conv_ref, pw_ref,
                       ps_ref, wout_ref, o_ref, z_ref, a_ref, *, seq_len):
    i = pl.program_id(1)
    n_i = pl.num_programs(1)
    ts = x_ref.shape[1]
    h0 = MIX_HALO
    aw = conv_ref.shape[1]
    bg = pw_ref.shape[1]
    xm = x_ref[0]
    xa = jnp.concatenate([xp_ref[0], xm, xn_ref[0]], axis=0)
    hn = _rms(xa, g_ref[...]).astype(BF16)
    z = _dot(hn, win_ref[...])
    row = lax.broadcasted_iota(jnp.int32, (ts + 2 * h0, 1), 0)
    inside = ((row >= h0) | (i > 0)) & ((row < ts + h0) | (i < n_i - 1))
    z_ref[...] = jnp.where(inside, z, 0.0)

    a_ref[...] = z_ref[:, 2 * aw:3 * aw] * z_ref[:, 0:aw]
    cw = conv_ref[...]
    conv = (cw[0:1] * a_ref[h0 - 1:h0 - 1 + ts, :] + cw[1:2] * a_ref[h0:h0 + ts, :]
            + cw[2:3] * a_ref[h0 + 1:h0 + 1 + ts, :])
    parts = [(z_ref[h0:h0 + ts, aw:2 * aw] * conv).astype(BF16)]

    pos = i * ts + lax.broadcasted_iota(jnp.int32, (ts, 1), 0)
    for gi, win in enumerate(POOL_WINDOWS):
        left = win // 2
        right = win - 1 - left
        c0 = 3 * aw + gi * bg
        s = z_ref[h0 - left:h0 - left + ts, c0:c0 + bg]
        for j in range(-left + 1, right + 1):
            s = s + z_ref[h0 + j:h0 + j + ts, c0:c0 + bg]
        cnt = (jnp.minimum(pos + right, seq_len - 1) - jnp.maximum(pos - left, 0) + 1).astype(F32)
        p = s / cnt - z_ref[h0:h0 + ts, c0:c0 + bg]
        yb = _dot(p.astype(BF16), pw_ref[gi]) * ps_ref[:, gi * bg:(gi + 1) * bg]
        parts.append(yb.astype(BF16))
    y = jnp.concatenate(parts, axis=1)
    o_ref[0] = xm + _dot(y, wout_ref[...])


def _even_mixer(x, g, w_in, conv_a, pool_w, pool_scale, w_out):
    b, s, d = x.shape
    ts, h0 = SEQ_TILE, MIX_HALO
    r = ts // h0
    incols = w_in.shape[1]
    aw = conv_a.shape[1]
    return pl.pallas_call(
        functools.partial(_even_mixer_kernel, seq_len=s),
        grid=(b, s // ts),
        in_specs=[
            pl.BlockSpec((1, ts, d), lambda bi, i: (bi, i, 0)),
            pl.BlockSpec((1, h0, d), lambda bi, i: (bi, jnp.maximum(i * r - 1, 0), 0)),
            pl.BlockSpec((1, h0, d), lambda bi, i: (bi, jnp.minimum((i + 1) * r, s // h0 - 1), 0)),
            _const_spec((1, d)),
            _const_spec((d, incols)),
            _const_spec(conv_a.shape),
            _const_spec(pool_w.shape),
            _const_spec((1, pool_scale.shape[-1])),
            _const_spec(w_out.shape),
        ],
        out_specs=pl.BlockSpec((1, ts, d), lambda bi, i: (bi, i, 0)),
        out_shape=jax.ShapeDtypeStruct((b, s, d), F32),
        scratch_shapes=[pltpu.VMEM((ts + 2 * h0, incols), F32),
                        pltpu.VMEM((ts + 2 * h0, aw), F32)],
        compiler_params=_cparams("parallel", "parallel"),
        name="even_mixer",
    )(x, x, x, g.reshape(1, d), w_in.astype(BF16), conv_a, pool_w.astype(BF16),
      pool_scale.reshape(1, -1), w_out.astype(BF16))


def _conformer_kernel(x_ref, xp_ref, xn_ref, g_ref, pw1_ref, b1_ref, dw_ref, dwb_ref,
                      lng_ref, lnb_ref, pw2_ref, b2_ref, o_ref, glu_ref, act_ref):
    i = pl.program_id(1)
    n_i = pl.num_programs(1)
    ts = x_ref.shape[1]
    d = x_ref.shape[2]
    h0 = CONF_HALO
    taps = dw_ref.shape[0] // SUBLANES
    half = taps // 2
    xm = x_ref[0]
    xa = jnp.concatenate([xp_ref[0], xm, xn_ref[0]], axis=0)
    hn = _rms(xa, g_ref[...]).astype(BF16)
    y1 = _dot(hn, pw1_ref[...]) + b1_ref[...]
    glu = y1[:, :d] * jax.nn.sigmoid(y1[:, d:])
    row = lax.broadcasted_iota(jnp.int32, (ts + 2 * h0, 1), 0)
    inside = ((row >= h0) | (i > 0)) & ((row < ts + h0) | (i < n_i - 1))
    glu_ref[0] = jnp.where(inside, glu, 0.0)
    span = ts + 2 * h0 - SUBLANES
    for r in range(1, SUBLANES):
        glu_ref[r, 0:span, :] = glu_ref[0, r:r + span, :]

    def conv_block(rb, carry):
        base = pl.multiple_of(rb * CONV_ROWS, CONV_ROWS)
        acc = jnp.zeros((CONV_ROWS, d), F32) + dwb_ref[...]
        for k in range(taps):
            off = h0 - half + k
            wk = jnp.tile(dw_ref[SUBLANES * k:SUBLANES * (k + 1), :], (CONV_ROWS // SUBLANES, 1))
            acc = acc + wk * glu_ref[off % SUBLANES,
                                     pl.ds(base + (off // SUBLANES) * SUBLANES, CONV_ROWS), :]
        mu = jnp.mean(acc, axis=-1, keepdims=True)
        xc = acc - mu
        var = jnp.mean(xc * xc, axis=-1, keepdims=True)
        yn = xc * lax.rsqrt(var + EPS) * lng_ref[...] + lnb_ref[...]
        act_ref[pl.ds(base, CONV_ROWS), :] = (yn * jax.nn.sigmoid(yn)).astype(BF16)
        return carry

    lax.fori_loop(0, ts // CONV_ROWS, conv_block, 0)
    o_ref[0] = xm + _dot(act_ref[...], pw2_ref[...]) + b2_ref[...]


def _conformer(x, g, pw1_w, pw1_b, dw_w, dw_b, ln_g, ln_b, pw2_w, pw2_b):
    b, s, d = x.shape
    ts, h0 = SEQ_TILE, CONF_HALO
    r = ts // h0
    return pl.pallas_call(
        _conformer_kernel,
        grid=(b, s // ts),
        in_specs=[
            pl.BlockSpec((1, ts, d), lambda bi, i: (bi, i, 0)),
            pl.BlockSpec((1, h0, d), lambda bi, i: (bi, jnp.maximum(i * r - 1, 0), 0)),
            pl.BlockSpec((1, h0, d), lambda bi, i: (bi, jnp.minimum((i + 1) * r, s // h0 - 1), 0)),
            _const_spec((1, d)),
            _const_spec(pw1_w.shape),
            _const_spec((1, 2 * d)),
            _const_spec((dw_w.shape[0] * SUBLANES, d)),
            _const_spec((1, d)),
            _const_spec((1, d)),
            _const_spec((1, d)),
            _const_spec(pw2_w.shape),
            _const_spec((1, d)),
        ],
        out_specs=pl.BlockSpec((1, ts, d), lambda bi, i: (bi, i, 0)),
        out_shape=jax.ShapeDtypeStruct((b, s, d), F32),
        scratch_shapes=[pltpu.VMEM((SUBLANES, ts + 2 * h0, d), F32), pltpu.VMEM((ts, d), BF16)],
        compiler_params=_cparams("parallel", "parallel"),
        name="conformer",
    )(x, x, x, g.reshape(1, d), pw1_w.astype(BF16), pw1_b.reshape(1, -1),
      jnp.repeat(dw_w, SUBLANES, axis=0),
      dw_b.reshape(1, d), ln_g.reshape(1, d), ln_b.reshape(1, d), pw2_w.astype(BF16),
      pw2_b.reshape(1, d))


def _kv_kernel(mem_ref, g_ref, wkv_ref, kv_ref):
    mn = _rms(mem_ref[0], g_ref[0]).astype(BF16)
    kv_ref[0, 0] = _dot(mn, wkv_ref[0]).astype(BF16)


def _memory_kv(mem, norm_mem, wkv):
    b, m, d = mem.shape
    nl = wkv.shape[0]
    return pl.pallas_call(
        _kv_kernel,
        grid=(nl, b),
        in_specs=[
            pl.BlockSpec((1, m, d), lambda l, bi: (bi, 0, 0)),
            pl.BlockSpec((1, 1, d), lambda l, bi: (l, 0, 0)),
            pl.BlockSpec((1, d, 2 * d), lambda l, bi: (l, 0, 0)),
        ],
        out_specs=pl.BlockSpec((1, 1, m, 2 * d), lambda l, bi: (l, bi, 0, 0)),
        out_shape=jax.ShapeDtypeStruct((nl, b, m, 2 * d), BF16),
        compiler_params=_cparams("parallel", "parallel"),
        name="memory_kv",
    )(mem, norm_mem.reshape(nl, 1, d), wkv.astype(BF16))


def _xattn_kernel(h_ref, g_ref, wq_ref, kv_ref, wo_ref, o_ref):
    d = h_ref.shape[2]
    hd = d // XATTN_HEADS
    hb = h_ref[0]
    hn = _rms(hb, g_ref[...]).astype(BF16)
    q = (_dot(hn, wq_ref[...]) * (hd ** -0.5)).astype(BF16)
    kv = kv_ref[0]
    outs = []
    for a in range(XATTN_HEADS):
        qh = q[:, a * hd:(a + 1) * hd]
        kh = kv[:, a * hd:(a + 1) * hd]
        vh = kv[:, d + a * hd:d + (a + 1) * hd]
        sc = lax.dot_general(qh, kh, (((1,), (1,)), ((), ())), preferred_element_type=F32)
        e = jnp.exp(sc - jnp.max(sc, axis=-1, keepdims=True))
        o = _dot(e.astype(BF16), vh) / jnp.sum(e, axis=-1, keepdims=True)
        outs.append(o.astype(BF16))
    o_ref[0] = hb + _dot(jnp.concatenate(outs, axis=1), wo_ref[...])


def _cross_attn(h, g, wq, kv, wo):
    b, s, d = h.shape
    ts = SEQ_TILE
    m = kv.shape[1]
    return pl.pallas_call(
        _xattn_kernel,
        grid=(b, s // ts),
        in_specs=[
            pl.BlockSpec((1, ts, d), lambda bi, i: (bi, i, 0)),
            _const_spec((1, d)),
            _const_spec((d, d)),
            pl.BlockSpec((1, m, 2 * d), lambda bi, i: (bi, 0, 0)),
            _const_spec((d, d)),
        ],
        out_specs=pl.BlockSpec((1, ts, d), lambda bi, i: (bi, i, 0)),
        out_shape=jax.ShapeDtypeStruct((b, s, d), F32),
        compiler_params=_cparams("parallel", "parallel"),
        name="cross_attn",
    )(h, g.reshape(1, d), wq.astype(BF16), kv, wo.astype(BF16))


def _swiglu_kernel(h_ref, g_ref, wgu_ref, wd_ref, o_ref):
    ff = wd_ref.shape[0]
    hb = h_ref[...]
    hn = _rms(hb, g_ref[...]).astype(BF16)
    acc = hb
    for c in range(ff // FFN_CHUNK):
        lo = c * FFN_CHUNK
        gate = _dot(hn, wgu_ref[:, lo:lo + FFN_CHUNK])
        up = _dot(hn, wgu_ref[:, ff + lo:ff + lo + FFN_CHUNK])
        act = (gate * jax.nn.sigmoid(gate) * up).astype(BF16)
        acc = acc + _dot(act, wd_ref[lo:lo + FFN_CHUNK, :])
    o_ref[...] = acc


def _dense_swiglu(h2, g, w_gu, w_down):
    t, d = h2.shape
    ts = SEQ_TILE
    return pl.pallas_call(
        _swiglu_kernel,
        grid=(t // ts,),
        in_specs=[
            pl.BlockSpec((ts, d), lambda i: (i, 0)),
            _const_spec((1, d)),
            _const_spec(w_gu.shape),
            _const_spec(w_down.shape),
        ],
        out_specs=pl.BlockSpec((ts, d), lambda i: (i, 0)),
        out_shape=jax.ShapeDtypeStruct((t, d), F32),
        compiler_params=_cparams("parallel"),
        name="dense_swiglu",
    )(h2, g.reshape(1, d), w_gu.astype(BF16), w_down.astype(BF16))


def _router_kernel(h_ref, g_ref, wr_ref, hn_ref, meta_ref, before_ref, total_ref, carry_ref):
    i = pl.program_id(0)
    tc = h_ref.shape[0]

    @pl.when(i == 0)
    def _():
        carry_ref[...] = jnp.zeros_like(carry_ref)

    hn = _rms(h_ref[...], g_ref[...])
    hn_ref[...] = hn.astype(BF16)
    logits = jnp.dot(hn, wr_ref[...], preferred_element_type=F32,
                     precision=lax.Precision.HIGHEST)
    lane = lax.broadcasted_iota(jnp.int32, (tc, LANES), 1)
    neg = jnp.float32(-jnp.inf)
    l1 = jnp.where(lane < N_EXPERTS, logits, neg)
    m1 = jnp.max(l1, axis=-1, keepdims=True)
    i1 = jnp.min(jnp.where(l1 == m1, lane, LANES), axis=-1, keepdims=True)
    l2 = jnp.where(lane == i1, neg, l1)
    m2 = jnp.max(l2, axis=-1, keepdims=True)
    i2 = jnp.min(jnp.where(l2 == m2, lane, LANES), axis=-1, keepdims=True)
    ex = jnp.exp(m2 - m1)
    w1 = 1.0 / (1.0 + ex)
    w2 = ex / (1.0 + ex)
    sel1 = lane == i1
    sel2 = lane == i2
    onehot = (sel1 | sel2).astype(BF16)
    r = lax.broadcasted_iota(jnp.int32, (tc, tc), 0)
    c = lax.broadcasted_iota(jnp.int32, (tc, tc), 1)
    lower = (c < r).astype(BF16)
    carry = carry_ref[...]
    prior = _dot(lower, onehot) + carry
    rank1 = jnp.sum(jnp.where(sel1, prior, 0.0), axis=-1, keepdims=True)
    rank2 = jnp.sum(jnp.where(sel2, prior, 0.0), axis=-1, keepdims=True)
    meta = jnp.where(lane == 0, i1.astype(F32), 0.0)
    meta = jnp.where(lane == 1, i2.astype(F32), meta)
    meta = jnp.where(lane == 2, rank1, meta)
    meta = jnp.where(lane == 3, rank2, meta)
    meta = jnp.where(lane == 4, w1, meta)
    meta = jnp.where(lane == 5, w2, meta)
    meta_ref[...] = meta
    before_ref[0] = carry
    new_carry = carry + jnp.sum(onehot.astype(F32), axis=0, keepdims=True)
    carry_ref[...] = new_carry
    total_ref[...] = new_carry


def _router(h2, g, router_w):
    t, d = h2.shape
    tc = TOK_CHUNK
    nc = t // tc
    wr = jnp.zeros((d, LANES), F32).at[:, :router_w.shape[1]].set(router_w)
    return pl.pallas_call(
        _router_kernel,
        grid=(nc,),
        in_specs=[
            pl.BlockSpec((tc, d), lambda i: (i, 0)),
            _const_spec((1, d)),
            _const_spec((d, LANES)),
        ],
        out_specs=[
            pl.BlockSpec((tc, d), lambda i: (i, 0)),
            pl.BlockSpec((tc, LANES), lambda i: (i, 0)),
            pl.BlockSpec((1, 1, LANES), lambda i: (i, 0, 0)),
            pl.BlockSpec((1, LANES), lambda i: (0, 0)),
        ],
        out_shape=[
            jax.ShapeDtypeStruct((t, d), BF16),
            jax.ShapeDtypeStruct((t, LANES), F32),
            jax.ShapeDtypeStruct((nc, 1, LANES), F32),
            jax.ShapeDtypeStruct((1, LANES), F32),
        ],
        scratch_shapes=[pltpu.VMEM((1, LANES), F32)],
        compiler_params=_cparams("arbitrary"),
        name="moe_router",
    )(h2, g.reshape(1, d), wr)


def _dispatch_kernel(wq_ref, wc_ref, wf_ref, dest_ref, hn_ref, xs_ref):
    w = pl.program_id(0)
    flags = wf_ref[w]
    rows, tc = xs_ref.shape[0], hn_ref.shape[0]

    @pl.when((flags & 2) != 0)
    def _():
        xs_ref[...] = jnp.zeros_like(xs_ref)

    @pl.when((flags & 1) != 0)
    def _():
        p = wq_ref[w] * rows + lax.broadcasted_iota(jnp.int32, (rows, tc), 0)
        d = dest_ref[0]
        onehot = ((d[0:1, :] == p) | (d[1:2, :] == p)).astype(BF16)
        xs_ref[...] += _dot(onehot, hn_ref[...]).astype(BF16)


def _dispatch(hn, dest_lanes, wq, wc, wf, n_rows):
    t, d = hn.shape
    tc = TOK_CHUNK
    return pl.pallas_call(
        _dispatch_kernel,
        grid_spec=pltpu.PrefetchScalarGridSpec(
            num_scalar_prefetch=3,
            grid=(wq.shape[0],),
            in_specs=[
                pl.BlockSpec((1, 2, tc), lambda w, wq, wc, wf: (wc[w], 0, 0)),
                pl.BlockSpec((tc, d), lambda w, wq, wc, wf: (wc[w], 0)),
            ],
            out_specs=pl.BlockSpec((MOE_ROWS, d), lambda w, wq, wc, wf: (wq[w], 0)),
        ),
        out_shape=jax.ShapeDtypeStruct((n_rows, d), BF16),
        compiler_params=_cparams("arbitrary"),
        name="moe_dispatch",
    )(wq, wc, wf, dest_lanes, hn)


def _experts_kernel(te_ref, nq_ref, xs_ref, wg_ref, wu_ref, wd_ref, ys_ref, acc_ref):
    q = pl.program_id(0)
    f = pl.program_id(1)
    live = q < nq_ref[0]

    @pl.when(live & (f == 0))
    def _():
        acc_ref[...] = jnp.zeros_like(acc_ref)

    @pl.when(live)
    def _():
        x = xs_ref[...]
        gate = _dot(x, wg_ref[0])
        up = _dot(x, wu_ref[0])
        act = (gate * jax.nn.sigmoid(gate) * up).astype(BF16)
        acc_ref[...] += _dot(act, wd_ref[0])

    @pl.when(live & (f == pl.num_programs(1) - 1))
    def _():
        ys_ref[...] = acc_ref[...].astype(BF16)


def _experts(xs, tile_expert, n_tiles, w_gu, w_down):
    n_rows, d = xs.shape
    ne, ff, _ = w_down.shape
    nf = ff // MOE_CHUNK
    nq = n_rows // MOE_ROWS

    def tile(q, nq_ref):
        return jnp.minimum(q, nq_ref[0] - 1)

    def chunk(q, f, nq_ref):
        return jnp.where(q < nq_ref[0], f, nf - 1)

    return pl.pallas_call(
        _experts_kernel,
        grid_spec=pltpu.PrefetchScalarGridSpec(
            num_scalar_prefetch=2,
            grid=(nq, nf),
            in_specs=[
                pl.BlockSpec((MOE_ROWS, d), lambda q, f, te, n: (tile(q, n), 0)),
                pl.BlockSpec((1, d, MOE_CHUNK),
                             lambda q, f, te, n: (te[tile(q, n)], 0, chunk(q, f, n))),
                pl.BlockSpec((1, d, MOE_CHUNK),
                             lambda q, f, te, n: (te[tile(q, n)], 0, nf + chunk(q, f, n))),
                pl.BlockSpec((1, MOE_CHUNK, d),
                             lambda q, f, te, n: (te[tile(q, n)], chunk(q, f, n), 0)),
            ],
            out_specs=pl.BlockSpec((MOE_ROWS, d), lambda q, f, te, n: (tile(q, n), 0)),
            scratch_shapes=[pltpu.VMEM((MOE_ROWS, d), F32)],
        ),
        out_shape=jax.ShapeDtypeStruct((n_rows, d), BF16),
        compiler_params=_cparams("arbitrary", "arbitrary"),
        name="moe_experts",
    )(tile_expert, n_tiles, xs, w_gu, w_gu, w_down)


def _combine_kernel(cq_ref, cc_ref, cf_ref, dest_ref, wts_ref, ys_ref, h_ref, g_ref,
                    o_ref, acc_ref):
    w = pl.program_id(0)
    flags = cf_ref[w]
    tc, rows = h_ref.shape[0], ys_ref.shape[0]

    @pl.when((flags & 2) != 0)
    def _():
        acc_ref[...] = jnp.zeros_like(acc_ref)

    @pl.when((flags & 1) != 0)
    def _():
        p = cq_ref[w] * rows + lax.broadcasted_iota(jnp.int32, (tc, rows), 1)
        d = dest_ref[...]
        wt = wts_ref[...]
        sel = (jnp.where(d[:, 0:1] == p, wt[:, 0:1], 0.0)
               + jnp.where(d[:, 1:2] == p, wt[:, 1:2], 0.0))
        acc_ref[...] += _dot(sel.astype(BF16), ys_ref[...])

    @pl.when((flags & 4) != 0)
    def _():
        o_ref[...] = _rms(h_ref[...] + acc_ref[...], g_ref[...])


def _combine(ys, dest, wts, h2, g_final, cq, cc, cf):
    t, d = h2.shape
    tc = TOK_CHUNK
    return pl.pallas_call(
        _combine_kernel,
        grid_spec=pltpu.PrefetchScalarGridSpec(
            num_scalar_prefetch=3,
            grid=(cq.shape[0],),
            in_specs=[
                pl.BlockSpec((tc, 2), lambda w, cq, cc, cf: (cc[w], 0)),
                pl.BlockSpec((tc, 2), lambda w, cq, cc, cf: (cc[w], 0)),
                pl.BlockSpec((MOE_ROWS, d), lambda w, cq, cc, cf: (cq[w], 0)),
                pl.BlockSpec((tc, d), lambda w, cq, cc, cf: (cc[w], 0)),
                pl.BlockSpec((1, d), lambda w, cq, cc, cf: (0, 0)),
            ],
            out_specs=pl.BlockSpec((tc, d), lambda w, cq, cc, cf: (cc[w], 0)),
            scratch_shapes=[pltpu.VMEM((tc, d), F32)],
        ),
        out_shape=jax.ShapeDtypeStruct((t, d), F32),
        compiler_params=_cparams("arbitrary"),
        name="moe_combine",
    )(cq, cc, cf, dest, wts, ys, h2, g_final.reshape(1, d))


def _pair_list(overlap, size):
    n_cols = overlap.shape[1]
    flat = overlap.reshape(-1)
    n_real = jnp.sum(flat.astype(jnp.int32))
    idx = jnp.nonzero(flat, size=size, fill_value=0)[0].astype(jnp.int32)
    k = jnp.arange(size, dtype=jnp.int32)
    real = k < n_real
    idx = jnp.where(real, idx, idx[jnp.maximum(n_real - 1, 0)])
    row = idx // n_cols
    col = idx % n_cols
    prev_row = jnp.concatenate([jnp.full((1,), -1, jnp.int32), row[:-1]])
    next_row = jnp.concatenate([row[1:], jnp.full((1,), -1, jnp.int32)])
    first = real & (row != prev_row)
    last = real & ((row != next_row) | (k == n_real - 1))
    return row, col, real, first, last


def _moe_block(h2, g, router_w, w_gu, w_down, g_final):
    t, d = h2.shape
    ne = w_gu.shape[0]
    tc = TOK_CHUNK
    nc = t // tc
    nq = (2 * t) // MOE_ROWS + ne
    n_rows = nq * MOE_ROWS

    hn, meta, before, total = _router(h2, g, router_w)

    experts = meta[:, 0:2].astype(jnp.int32)
    ranks = meta[:, 2:4].astype(jnp.int32)
    wts = meta[:, 4:6]
    counts = total[0, :ne].astype(jnp.int32)
    tiles_per = (counts + MOE_ROWS - 1) // MOE_ROWS
    tile_end = jnp.cumsum(tiles_per)
    tile_start = tile_end - tiles_per
    n_tiles = tile_end[-1]
    dest = jnp.take(tile_start * MOE_ROWS, experts) + ranks
    qs = jnp.arange(nq, dtype=jnp.int32)
    tile_expert = jnp.minimum(jnp.sum((qs[:, None] >= tile_end[None, :]).astype(jnp.int32), axis=1),
                              ne - 1)
    tile_local = qs - jnp.take(tile_start, tile_expert)
    rank_lo = tile_local * MOE_ROWS
    rank_hi = jnp.minimum(rank_lo + MOE_ROWS, jnp.take(counts, tile_expert))
    cum = jnp.concatenate([before[:, 0, :ne], total[:, :ne]], axis=0).astype(jnp.int32)
    cum_q = jnp.take(cum, tile_expert, axis=1).T
    overlap = ((cum_q[:, :-1] < rank_hi[:, None]) & (cum_q[:, 1:] > rank_lo[:, None])
               & (qs < n_tiles)[:, None])
    n_pairs = nq + ne * nc
    wq, wc, wreal, wfirst, _ = _pair_list(overlap, n_pairs)
    wf = wreal.astype(jnp.int32) + 2 * wfirst.astype(jnp.int32)
    cc, cq, creal, cfirst, clast = _pair_list(overlap.T, n_pairs)
    cf = creal.astype(jnp.int32) + 2 * cfirst.astype(jnp.int32) + 4 * clast.astype(jnp.int32)

    dest_lanes = dest.reshape(nc, tc, 2).transpose(0, 2, 1)
    xs = _dispatch(hn, dest_lanes, wq, wc, wf, n_rows)
    ys = _experts(xs, tile_expert, n_tiles.reshape(1), w_gu.astype(BF16), w_down.astype(BF16))
    return _combine(ys, dest, wts, h2, g_final, cq, cc, cf)


def kernel(x, mem, ev_norm_mix, ev_w_in, ev_conv_a, ev_pool_w, ev_pool_scale, ev_w_out, ev_norm_ffn, ev_ffn_gu, ev_ffn_down, od_norm_mix, od_pw1_w, od_pw1_b, od_dw_w, od_dw_b, od_ln_g, od_ln_b, od_pw2_w, od_pw2_b, od_norm_moe, od_router, od_moe_gu, od_moe_down, xa_norm, xa_norm_mem, xa_wq, xa_wkv, xa_wo, final_norm):
    b, s, d = x.shape
    assert ev_w_in.shape[0] == 1 and od_pw1_w.shape[0] == 1 and xa_wq.shape[0] == 2
    kv = _memory_kv(mem, xa_norm_mem, xa_wkv)
    h = _even_mixer(x, ev_norm_mix[0], ev_w_in[0], ev_conv_a[0], ev_pool_w[0],
                    ev_pool_scale[0], ev_w_out[0])
    h = _cross_attn(h, xa_norm[0], xa_wq[0], kv[0], xa_wo[0])
    h = _dense_swiglu(h.reshape(b * s, d), ev_norm_ffn[0], ev_ffn_gu[0],
                      ev_ffn_down[0]).reshape(b, s, d)
    h = _conformer(h, od_norm_mix[0], od_pw1_w[0], od_pw1_b[0], od_dw_w[0], od_dw_b[0],
                   od_ln_g[0], od_ln_b[0], od_pw2_w[0], od_pw2_b[0])
    h = _cross_attn(h, xa_norm[1], xa_wq[1], kv[1], xa_wo[1])
    out = _moe_block(h.reshape(b * s, d), od_norm_moe[0], od_router[0], od_moe_gu[0],
                     od_moe_down[0], final_norm)
    return out.reshape(b, s, d)
```

```python
import functools

import jax
import jax.numpy as jnp
from jax import lax
from jax.experimental import pallas as pl
from jax.experimental.pallas import tpu as pltpu

EPS = 1e-6
BF16 = jnp.bfloat16
F32 = jnp.float32

POOL_WINDOWS = (2, 4, 8, 16)
XATTN_HEADS = 4
N_EXPERTS = 8
LANES = 128
SUBLANES = 8
VMEM_LIMIT = 56 * 1024 * 1024

SEQ_TILE = 512
MIX_HALO = 8
CONF_HALO = 16
CONV_ROWS = 32
FFN_CHUNK = 256
MOE_ROWS = 512
MOE_CHUNK = 512
MOE_STEPS = 7
ROW_UNROLL = 4
TOK_CHUNK = 256


def _rms(x, g):
    return x * lax.rsqrt(jnp.mean(x * x, axis=-1, keepdims=True) + EPS) * g


def _dot(a, b):
    return jnp.dot(a, b, preferred_element_type=F32)


def _cparams(*sem):
    return pltpu.CompilerParams(dimension_semantics=sem, vmem_limit_bytes=VMEM_LIMIT)


def _const_spec(shape):
    nd = len(shape)
    return pl.BlockSpec(shape, lambda *_: (0,) * nd, pipeline_mode=pl.Buffered(1))


def _even_mixer_kernel(x_ref, xp_ref, xn_ref, g_ref, win_ref, conv_ref, pw_ref,
                       ps_ref, wout_ref, o_ref, z_ref, a_ref, *, seq_len):
    i = pl.program_id(1)
    n_i = pl.num_programs(1)
    ts = x_ref.shape[1]
    h0 = MIX_HALO
    aw = conv_ref.shape[1]
    bg = pw_ref.shape[1]
    xm = x_ref[0]
    xa = jnp.concatenate([xp_ref[0], xm, xn_ref[0]], axis=0)
    hn = _rms(xa, g_ref[...]).astype(BF16)
    z = _dot(hn, win_ref[...])
    row = lax.broadcasted_iota(jnp.int32, (ts + 2 * h0, 1), 0)
    inside = ((row >= h0) | (i > 0)) & ((row < ts + h0) | (i < n_i - 1))
    z_ref[...] = jnp.where(inside, z, 0.0)

    a_ref[...] = z_ref[:, 2 * aw:3 * aw] * z_ref[:, 0:aw]
    cw = conv_ref[...]
    conv = (cw[0:1] * a_ref[h0 - 1:h0 - 1 + ts, :] + cw[1:2] * a_ref[h0:h0 + ts, :]
            + cw[2:3] * a_ref[h0 + 1:h0 + 1 + ts, :])
    parts = [(z_ref[h0:h0 + ts, aw:2 * aw] * conv).astype(BF16)]

    pos = i * ts + lax.broadcasted_iota(jnp.int32, (ts, 1), 0)
    for gi, win in enumerate(POOL_WINDOWS):
        left = win // 2
        right = win - 1 - left
        c0 = 3 * aw + gi * bg
        s = z_ref[h0 - left:h0 - left + ts, c0:c0 + bg]
        for j in range(-left + 1, right + 1):
            s = s + z_ref[h0 + j:h0 + j + ts, c0:c0 + bg]
        cnt = (jnp.minimum(pos + right, seq_len - 1) - jnp.maximum(pos - left, 0) + 1).astype(F32)
        p = s / cnt - z_ref[h0:h0 + ts, c0:c0 + bg]
        yb = _dot(p.astype(BF16), pw_ref[gi]) * ps_ref[:, gi * bg:(gi + 1) * bg]
        parts.append(yb.astype(BF16))
    y = jnp.concatenate(parts, axis=1)
    o_ref[0] = xm + _dot(y, wout_ref[...])


def _even_mixer(x, g, w_in, conv_a, pool_w, pool_scale, w_out):
    b, s, d = x.shape
    ts, h0 = SEQ_TILE, MIX_HALO
    r = ts // h0
    incols = w_in.shape[1]
    aw = conv_a.shape[1]
    return pl.pallas_call(
        functools.partial(_even_mixer_kernel, seq_len=s),
        grid=(b, s // ts),
        in_specs=[
            pl.BlockSpec((1, ts, d), lambda bi, i: (bi, i, 0)),
            pl.BlockSpec((1, h0, d), lambda bi, i: (bi, jnp.maximum(i * r - 1, 0), 0)),
            pl.BlockSpec((1, h0, d), lambda bi, i: (bi, jnp.minimum((i + 1) * r, s // h0 - 1), 0)),
            _const_spec((1, d)),
            _const_spec((d, incols)),
            _const_spec(conv_a.shape),
            _const_spec(pool_w.shape),
            _const_spec((1, pool_scale.shape[-1])),
            _const_spec(w_out.shape),
        ],
        out_specs=pl.BlockSpec((1, ts, d), lambda bi, i: (bi, i, 0)),
        out_shape=jax.ShapeDtypeStruct((b, s, d), F32),
        scratch_shapes=[pltpu.VMEM((ts + 2 * h0, incols), F32),
                        pltpu.VMEM((ts + 2 * h0, aw), F32)],
        compiler_params=_cparams("parallel", "parallel"),
        name="even_mixer",
    )(x, x, x, g.reshape(1, d), w_in.astype(BF16), conv_a, pool_w.astype(BF16),
      pool_scale.reshape(1, -1), w_out.astype(BF16))


def _conformer_kernel(x_ref, xp_ref, xn_ref, g_ref, pw1_ref, b1_ref, dw_ref, dwb_ref,
                      lng_ref, lnb_ref, pw2_ref, b2_ref, o_ref, glu_ref, conv_ref):
    i = pl.program_id(1)
    n_i = pl.num_programs(1)
    ts = x_ref.shape[1]
    d = x_ref.shape[2]
    h0 = CONF_HALO
    taps = dw_ref.shape[0] // SUBLANES
    half = taps // 2
    xm = x_ref[0]
    xa = jnp.concatenate([xp_ref[0], xm, xn_ref[0]], axis=0)
    hn = _rms(xa, g_ref[...]).astype(BF16)
    y1 = _dot(hn, pw1_ref[...]) + b1_ref[...]
    glu = y1[:, :d] * jax.nn.sigmoid(y1[:, d:])
    row = lax.broadcasted_iota(jnp.int32, (ts + 2 * h0, 1), 0)
    inside = ((row >= h0) | (i > 0)) & ((row < ts + h0) | (i < n_i - 1))
    glu_ref[0] = jnp.where(inside, glu, 0.0)
    span = ts + 2 * h0 - SUBLANES
    for r in range(1, SUBLANES):
        glu_ref[r, 0:span, :] = glu_ref[0, r:r + span, :]

    n_sub = CONV_ROWS // SUBLANES
    for c in range(d // LANES):
        cols = slice(c * LANES, (c + 1) * LANES)
        w_taps = [dw_ref[SUBLANES * k:SUBLANES * (k + 1), cols] for k in range(taps)]
        bias = dwb_ref[:, cols]

        def conv_block(rb, carry, cols=cols, w_taps=w_taps, bias=bias):
            base = pl.multiple_of(rb * CONV_ROWS, CONV_ROWS)
            accs = [jnp.zeros((SUBLANES, LANES), F32) + bias for _ in range(n_sub)]
            for k in range(taps):
                off = h0 - half + k
                start = base + (off // SUBLANES) * SUBLANES
                for a in range(n_sub):
                    accs[a] = accs[a] + w_taps[k] * glu_ref[off % SUBLANES,
                                                            pl.ds(start + a * SUBLANES, SUBLANES), cols]
            conv_ref[pl.ds(base, CONV_ROWS), cols] = jnp.concatenate(accs, axis=0)
            return carry

        lax.fori_loop(0, ts // CONV_ROWS, conv_block, 0)

    acc = conv_ref[...]
    mu = jnp.mean(acc, axis=-1, keepdims=True)
    xc = acc - mu
    var = jnp.mean(xc * xc, axis=-1, keepdims=True)
    yn = xc * lax.rsqrt(var + EPS) * lng_ref[...] + lnb_ref[...]
    act = (yn * jax.nn.sigmoid(yn)).astype(BF16)
    o_ref[0] = xm + _dot(act, pw2_ref[...]) + b2_ref[...]


def _conformer(x, g, pw1_w, pw1_b, dw_w, dw_b, ln_g, ln_b, pw2_w, pw2_b):
    b, s, d = x.shape
    ts, h0 = SEQ_TILE, CONF_HALO
    r = ts // h0
    return pl.pallas_call(
        _conformer_kernel,
        grid=(b, s // ts),
        in_specs=[
            pl.BlockSpec((1, ts, d), lambda bi, i: (bi, i, 0)),
            pl.BlockSpec((1, h0, d), lambda bi, i: (bi, jnp.maximum(i * r - 1, 0), 0)),
            pl.BlockSpec((1, h0, d), lambda bi, i: (bi, jnp.minimum((i + 1) * r, s // h0 - 1), 0)),
            _const_spec((1, d)),
            _const_spec(pw1_w.shape),
            _const_spec((1, 2 * d)),
            _const_spec((dw_w.shape[0] * SUBLANES, d)),
            _const_spec((1, d)),
            _const_spec((1, d)),
            _const_spec((1, d)),
            _const_spec(pw2_w.shape),
            _const_spec((1, d)),
        ],
        out_specs=pl.BlockSpec((1, ts, d), lambda bi, i: (bi, i, 0)),
        out_shape=jax.ShapeDtypeStruct((b, s, d), F32),
        scratch_shapes=[pltpu.VMEM((SUBLANES, ts + 2 * h0, d), F32), pltpu.VMEM((ts, d), F32)],
        compiler_params=_cparams("parallel", "parallel"),
        name="conformer",
    )(x, x, x, g.reshape(1, d), pw1_w.astype(BF16), pw1_b.reshape(1, -1),
      jnp.repeat(dw_w, SUBLANES, axis=0),
      dw_b.reshape(1, d), ln_g.reshape(1, d), ln_b.reshape(1, d), pw2_w.astype(BF16),
      pw2_b.reshape(1, d))


def _kv_kernel(mem_ref, g_ref, wkv_ref, kv_ref):
    mn = _rms(mem_ref[0], g_ref[0]).astype(BF16)
    kv_ref[0, 0] = _dot(mn, wkv_ref[0]).astype(BF16)


def _memory_kv(mem, norm_mem, wkv):
    b, m, d = mem.shape
    nl = wkv.shape[0]
    return pl.pallas_call(
        _kv_kernel,
        grid=(nl, b),
        in_specs=[
            pl.BlockSpec((1, m, d), lambda l, bi: (bi, 0, 0)),
            pl.BlockSpec((1, 1, d), lambda l, bi: (l, 0, 0)),
            pl.BlockSpec((1, d, 2 * d), lambda l, bi: (l, 0, 0)),
        ],
        out_specs=pl.BlockSpec((1, 1, m, 2 * d), lambda l, bi: (l, bi, 0, 0)),
        out_shape=jax.ShapeDtypeStruct((nl, b, m, 2 * d), BF16),
        compiler_params=_cparams("parallel", "parallel"),
        name="memory_kv",
    )(mem, norm_mem.reshape(nl, 1, d), wkv.astype(BF16))


def _xattn_kernel(h_ref, g_ref, wq_ref, kv_ref, wo_ref, o_ref):
    d = h_ref.shape[2]
    hd = d // XATTN_HEADS
    hb = h_ref[0]
    hn = _rms(hb, g_ref[...]).astype(BF16)
    q = (_dot(hn, wq_ref[...]) * (hd ** -0.5)).astype(BF16)
    kv = kv_ref[0]
    outs = []
    for a in range(XATTN_HEADS):
        qh = q[:, a * hd:(a + 1) * hd]
        kh = kv[:, a * hd:(a + 1) * hd]
        vh = kv[:, d + a * hd:d + (a + 1) * hd]
        sc = lax.dot_general(qh, kh, (((1,), (1,)), ((), ())), preferred_element_type=F32)
        e = jnp.exp(sc - jnp.max(sc, axis=-1, keepdims=True))
        o = _dot(e.astype(BF16), vh) / jnp.sum(e, axis=-1, keepdims=True)
        outs.append(o.astype(BF16))
    o_ref[0] = hb + _dot(jnp.concatenate(outs, axis=1), wo_ref[...])


def _cross_attn(h, g, wq, kv, wo):
    b, s, d = h.shape
    ts = SEQ_TILE
    m = kv.shape[1]
    return pl.pallas_call(
        _xattn_kernel,
        grid=(b, s // ts),
        in_specs=[
            pl.BlockSpec((1, ts, d), lambda bi, i: (bi, i, 0)),
            _const_spec((1, d)),
            _const_spec((d, d)),
            pl.BlockSpec((1, m, 2 * d), lambda bi, i: (bi, 0, 0)),
            _const_spec((d, d)),
        ],
        out_specs=pl.BlockSpec((1, ts, d), lambda bi, i: (bi, i, 0)),
        out_shape=jax.ShapeDtypeStruct((b, s, d), F32),
        compiler_params=_cparams("parallel", "parallel"),
        name="cross_attn",
    )(h, g.reshape(1, d), wq.astype(BF16), kv, wo.astype(BF16))


def _swiglu_kernel(h_ref, g_ref, wgu_ref, wd_ref, o_ref):
    ff = wd_ref.shape[0]
    hb = h_ref[...]
    hn = _rms(hb, g_ref[...]).astype(BF16)
    acc = hb
    for c in range(ff // FFN_CHUNK):
        lo = c * FFN_CHUNK
        gate = _dot(hn, wgu_ref[:, lo:lo + FFN_CHUNK])
        up = _dot(hn, wgu_ref[:, ff + lo:ff + lo + FFN_CHUNK])
        act = (gate * jax.nn.sigmoid(gate) * up).astype(BF16)
        acc = acc + _dot(act, wd_ref[lo:lo + FFN_CHUNK, :])
    o_ref[...] = acc


def _dense_swiglu(h2, g, w_gu, w_down):
    t, d = h2.shape
    ts = SEQ_TILE
    return pl.pallas_call(
        _swiglu_kernel,
        grid=(t // ts,),
        in_specs=[
            pl.BlockSpec((ts, d), lambda i: (i, 0)),
            _const_spec((1, d)),
            _const_spec(w_gu.shape),
            _const_spec(w_down.shape),
        ],
        out_specs=pl.BlockSpec((ts, d), lambda i: (i, 0)),
        out_shape=jax.ShapeDtypeStruct((t, d), F32),
        compiler_params=_cparams("parallel"),
        name="dense_swiglu",
    )(h2, g.reshape(1, d), w_gu.astype(BF16), w_down.astype(BF16))


def _store_row_tiles(ref, x):
    for s in range(SUBLANES):
        ref[pl.ds(s, x.shape[0], stride=SUBLANES), :] = x[:, s * LANES:(s + 1) * LANES]


def _load_row_tiles(ref):
    rows = ref.shape[0] // SUBLANES
    return jnp.concatenate([ref[pl.ds(s, rows, stride=SUBLANES), :] for s in range(SUBLANES)], axis=1)


def _row_tile(ref, r):
    return ref.at[pl.ds(pl.multiple_of(r * SUBLANES, SUBLANES), SUBLANES), :]


def _router_kernel(h_ref, g_ref, wr_ref, hn_ref, meta_ref, total_ref, carry_ref):
    i = pl.program_id(0)
    tc = h_ref.shape[0]

    @pl.when(i == 0)
    def _():
        carry_ref[...] = jnp.zeros_like(carry_ref)

    @pl.when(i == pl.num_programs(0) - 1)
    def _():
        hn_ref[...] = jnp.zeros_like(hn_ref)

    @pl.when(i < pl.num_programs(0) - 1)
    def _():
        _route_chunk(h_ref, g_ref, wr_ref, hn_ref, meta_ref, total_ref, carry_ref)


def _route_chunk(h_ref, g_ref, wr_ref, hn_ref, meta_ref, total_ref, carry_ref):
    tc = h_ref.shape[0]
    hn = _rms(h_ref[...], g_ref[...])
    _store_row_tiles(hn_ref, hn)
    logits = jnp.dot(hn, wr_ref[...], preferred_element_type=F32,
                     precision=lax.Precision.HIGHEST)
    lane = lax.broadcasted_iota(jnp.int32, (tc, LANES), 1)
    neg = jnp.float32(-jnp.inf)
    l1 = jnp.where(lane < N_EXPERTS, logits, neg)
    m1 = jnp.max(l1, axis=-1, keepdims=True)
    i1 = jnp.min(jnp.where(l1 == m1, lane, LANES), axis=-1, keepdims=True)
    l2 = jnp.where(lane == i1, neg, l1)
    m2 = jnp.max(l2, axis=-1, keepdims=True)
    i2 = jnp.min(jnp.where(l2 == m2, lane, LANES), axis=-1, keepdims=True)
    ex = jnp.exp(m2 - m1)
    w1 = 1.0 / (1.0 + ex)
    w2 = ex / (1.0 + ex)
    sel1 = lane == i1
    sel2 = lane == i2
    onehot = (sel1 | sel2).astype(BF16)
    r = lax.broadcasted_iota(jnp.int32, (tc, tc), 0)
    c = lax.broadcasted_iota(jnp.int32, (tc, tc), 1)
    lower = (c < r).astype(BF16)
    carry = carry_ref[...]
    prior = _dot(lower, onehot) + carry
    rank1 = jnp.sum(jnp.where(sel1, prior, 0.0), axis=-1, keepdims=True)
    rank2 = jnp.sum(jnp.where(sel2, prior, 0.0), axis=-1, keepdims=True)
    meta = jnp.where(lane == 0, i1.astype(F32), 0.0)
    meta = jnp.where(lane == 1, i2.astype(F32), meta)
    meta = jnp.where(lane == 2, rank1, meta)
    meta = jnp.where(lane == 3, rank2, meta)
    meta = jnp.where(lane == 4, w1, meta)
    meta = jnp.where(lane == 5, w2, meta)
    meta_ref[...] = meta
    new_carry = carry + jnp.sum(onehot.astype(F32), axis=0, keepdims=True)
    carry_ref[...] = new_carry
    total_ref[...] = new_carry


def _router(h2, g, router_w):
    t, d = h2.shape
    tc = TOK_CHUNK
    nc = t // tc
    wr = jnp.zeros((d, LANES), F32).at[:, :router_w.shape[1]].set(router_w)
    last = nc - 1
    return pl.pallas_call(
        _router_kernel,
        grid=(nc + 1,),
        in_specs=[
            pl.BlockSpec((tc, d), lambda i: (jnp.minimum(i, last), 0)),
            _const_spec((1, d)),
            _const_spec((d, LANES)),
        ],
        out_specs=[
            pl.BlockSpec((tc * SUBLANES, LANES), lambda i: (i, 0)),
            pl.BlockSpec((tc, LANES), lambda i: (jnp.minimum(i, last), 0)),
            pl.BlockSpec((1, LANES), lambda i: (0, 0)),
        ],
        out_shape=[
            jax.ShapeDtypeStruct(((t + tc) * SUBLANES, LANES), F32),
            jax.ShapeDtypeStruct((t, LANES), F32),
            jax.ShapeDtypeStruct((1, LANES), F32),
        ],
        scratch_shapes=[pltpu.VMEM((1, LANES), F32)],
        compiler_params=_cparams("arbitrary"),
        name="moe_router",
    )(h2, g.reshape(1, d), wr)


def _row_copy(src, dst, sem):
    return pltpu.make_async_copy(src, dst, sem)


def _experts_kernel(te_ref, nt_ref, nreal_ref, inv_cur_ref, inv_nxt_ref, inv_prv_ref, hn_hbm,
                    wg_ref, wu_ref, wd_ref, y_hbm, xin_ref, x_ref, acc_ref, yout_ref,
                    gsem, ssem, *, n_tokens):
    q = pl.program_id(0)
    f = pl.program_id(1)
    nf = pl.num_programs(1)
    nt = nt_ref[0]
    rows = x_ref.shape[0]
    live = q < nt
    slot = q % 2
    other = 1 - slot
    share = -(-rows // MOE_STEPS)
    n_prev = jnp.where(q >= 1, nreal_ref[jnp.maximum(q - 1, 0)], 0)
    n_prev2 = nreal_ref[jnp.maximum(q - 2, 0)]

    def gather_one(idx_ref, dst_slot, j):
        a = idx_ref[0, 0, j]
        src = jnp.where(a < 0, n_tokens, jnp.where(a >= n_tokens, a - n_tokens, a))
        _row_copy(_row_tile(hn_hbm, src), _row_tile(xin_ref.at[dst_slot], j), gsem.at[dst_slot]).start()

    def scatter_one(src_slot, j):
        _row_copy(_row_tile(yout_ref.at[src_slot], j), _row_tile(y_hbm, inv_prv_ref[0, 0, j]),
                  ssem.at[src_slot]).start()

    def wait_gathered(s):
        _row_copy(hn_hbm.at[pl.ds(0, rows * SUBLANES), :], xin_ref.at[s], gsem.at[s]).wait()

    def wait_scattered(s, n):
        size = pl.multiple_of(n * SUBLANES, SUBLANES)
        _row_copy(yout_ref.at[s, pl.ds(0, size), :], y_hbm.at[pl.ds(0, size), :], ssem.at[s]).wait()

    @pl.when((q == 0) & (f == 0))
    def _():
        def one(j, c):
            gather_one(inv_cur_ref, 0, j)
            return c
        lax.fori_loop(0, rows, one, 0, unroll=ROW_UNROLL)

    @pl.when((q >= 2) & (q <= nt) & (f == 0))
    def _():
        wait_scattered(slot, n_prev2)

    @pl.when((q <= nt) & (f == 0))
    def _():
        wait_gathered(slot)

    @pl.when(live & (f == 0))
    def _():
        x_ref[...] = _load_row_tiles(xin_ref.at[slot]).astype(BF16)
        acc_ref[...] = jnp.zeros_like(acc_ref)

    @pl.when(live)
    def _():
        for i in range(share):
            j = f * share + i
            jc = jnp.minimum(j, rows - 1)

            @pl.when(j < rows)
            def _():
                gather_one(inv_nxt_ref, other, jc)

            @pl.when(j < n_prev)
            def _():
                scatter_one(other, jc)

        x = x_ref[...]
        gate = _dot(x, wg_ref[0])
        up = _dot(x, wu_ref[0])
        act = (gate * jax.nn.sigmoid(gate) * up).astype(BF16)
        acc_ref[...] += _dot(act, wd_ref[0])

    @pl.when(live & (f == nf - 1))
    def _():
        _store_row_tiles(yout_ref.at[slot], acc_ref[...])

    @pl.when((q == nt) & (f == 0))
    def _():
        def one(j, c):
            scatter_one(other, j)
            return c
        lax.fori_loop(0, n_prev, one, 0)
        wait_scattered(other, n_prev)


def _experts(hn_rows, inv, tile_expert, n_tiles, tile_real, w_gu, w_down, n_tokens):
    d = SUBLANES * LANES
    ne, ff, _ = w_down.shape
    assert w_down.shape[2] == d
    nf = ff // MOE_CHUNK
    assert nf == MOE_STEPS
    nq = inv.shape[0]

    def tile(q, nt):
        return jnp.minimum(q, nt[0] - 1)

    def chunk(q, f, nt):
        return jnp.where(q < nt[0], f, nf - 1)

    idx_block = (1, 1, MOE_ROWS)
    return pl.pallas_call(
        functools.partial(_experts_kernel, n_tokens=n_tokens),
        grid_spec=pltpu.PrefetchScalarGridSpec(
            num_scalar_prefetch=3,
            grid=(nq + 1, nf),
            in_specs=[
                pl.BlockSpec(idx_block, lambda q, f, te, nt, nr: (jnp.minimum(q, nq - 1), 0, 0),
                             memory_space=pltpu.SMEM),
                pl.BlockSpec(idx_block, lambda q, f, te, nt, nr: (jnp.minimum(q + 1, nq - 1), 0, 0),
                             memory_space=pltpu.SMEM),
                pl.BlockSpec(idx_block, lambda q, f, te, nt, nr: (jnp.clip(q - 1, 0, nq - 1), 0, 0),
                             memory_space=pltpu.SMEM),
                pl.BlockSpec(memory_space=pl.ANY),
                pl.BlockSpec((1, d, MOE_CHUNK),
                             lambda q, f, te, nt, nr: (te[tile(q, nt)], 0, chunk(q, f, nt))),
                pl.BlockSpec((1, d, MOE_CHUNK),
                             lambda q, f, te, nt, nr: (te[tile(q, nt)], 0, nf + chunk(q, f, nt))),
                pl.BlockSpec((1, MOE_CHUNK, d),
                             lambda q, f, te, nt, nr: (te[tile(q, nt)], chunk(q, f, nt), 0)),
            ],
            out_specs=pl.BlockSpec(memory_space=pl.ANY),
            scratch_shapes=[
                pltpu.VMEM((2, MOE_ROWS * SUBLANES, LANES), F32),
                pltpu.VMEM((MOE_ROWS, d), BF16),
                pltpu.VMEM((MOE_ROWS, d), F32),
                pltpu.VMEM((2, MOE_ROWS * SUBLANES, LANES), F32),
                pltpu.SemaphoreType.DMA((2,)),
                pltpu.SemaphoreType.DMA((2,)),
            ],
        ),
        out_shape=jax.ShapeDtypeStruct((2 * n_tokens * SUBLANES, LANES), F32),
        compiler_params=_cparams("arbitrary", "arbitrary"),
        name="moe_experts",
    )(tile_expert, n_tiles, tile_real, inv, inv, inv, hn_rows, w_gu, w_gu, w_down)


def _combine_kernel(h_ref, y1_ref, y2_ref, wts_ref, g_ref, o_ref):
    wt = wts_ref[...]
    moe = wt[:, 0:1] * _load_row_tiles(y1_ref) + wt[:, 1:2] * _load_row_tiles(y2_ref)
    o_ref[...] = _rms(h_ref[...] + moe, g_ref[...])


def _combine(h2, y, wts, g_final):
    t, d = h2.shape
    ts = SEQ_TILE
    nt = t // ts
    return pl.pallas_call(
        _combine_kernel,
        grid=(nt,),
        in_specs=[
            pl.BlockSpec((ts, d), lambda i: (i, 0)),
            pl.BlockSpec((ts * SUBLANES, LANES), lambda i: (i, 0)),
            pl.BlockSpec((ts * SUBLANES, LANES), lambda i: (nt + i, 0)),
            pl.BlockSpec((ts, 2), lambda i: (i, 0)),
            _const_spec((1, d)),
        ],
        out_specs=pl.BlockSpec((ts, d), lambda i: (i, 0)),
        out_shape=jax.ShapeDtypeStruct((t, d), F32),
        compiler_params=_cparams("parallel"),
        name="moe_combine",
    )(h2, y, y, wts, g_final.reshape(1, d))


def _moe_block(h2, g, router_w, w_gu, w_down, g_final):
    t, d = h2.shape
    ne = w_gu.shape[0]
    nq = (2 * t) // MOE_ROWS + ne
    n_rows = nq * MOE_ROWS

    hn_rows, meta, total = _router(h2, g, router_w)

    experts = meta[:, 0:2].astype(jnp.int32)
    ranks = meta[:, 2:4].astype(jnp.int32)
    wts = meta[:, 4:6]
    counts = total[0, :ne].astype(jnp.int32)
    tiles_per = (counts + MOE_ROWS - 1) // MOE_ROWS
    tile_end = jnp.cumsum(tiles_per)
    tile_start = tile_end - tiles_per
    n_tiles = tile_end[-1]
    dest = (jnp.take(tile_start * MOE_ROWS, experts) + ranks).T.reshape(-1)
    qs = jnp.arange(nq, dtype=jnp.int32)
    tile_expert = jnp.minimum(jnp.sum((qs[:, None] >= tile_end[None, :]).astype(jnp.int32), axis=1),
                              ne - 1)
    tile_real = jnp.clip(jnp.take(counts, tile_expert) - (qs - jnp.take(tile_start, tile_expert)) * MOE_ROWS,
                         0, MOE_ROWS)
    inv = jnp.full((n_rows,), -1, jnp.int32).at[dest].set(jnp.arange(2 * t, dtype=jnp.int32))
    y = _experts(hn_rows, inv.reshape(nq, 1, MOE_ROWS), tile_expert, n_tiles.reshape(1), tile_real,
                 w_gu.astype(BF16), w_down.astype(BF16), t)
    return _combine(h2, y, wts, g_final)


def kernel(x, mem, ev_norm_mix, ev_w_in, ev_conv_a, ev_pool_w, ev_pool_scale, ev_w_out, ev_norm_ffn, ev_ffn_gu, ev_ffn_down, od_norm_mix, od_pw1_w, od_pw1_b, od_dw_w, od_dw_b, od_ln_g, od_ln_b, od_pw2_w, od_pw2_b, od_norm_moe, od_router, od_moe_gu, od_moe_down, xa_norm, xa_norm_mem, xa_wq, xa_wkv, xa_wo, final_norm):
    b, s, d = x.shape
    assert ev_w_in.shape[0] == 1 and od_pw1_w.shape[0] == 1 and xa_wq.shape[0] == 2
    kv = _memory_kv(mem, xa_norm_mem, xa_wkv)
    h = _even_mixer(x, ev_norm_mix[0], ev_w_in[0], ev_conv_a[0], ev_pool_w[0],
                    ev_pool_scale[0], ev_w_out[0])
    h = _cross_attn(h, xa_norm[0], xa_wq[0], kv[0], xa_wo[0])
    h = _dense_swiglu(h.reshape(b * s, d), ev_norm_ffn[0], ev_ffn_gu[0],
                      ev_ffn_down[0]).reshape(b, s, d)
    h = _conformer(h, od_norm_mix[0], od_pw1_w[0], od_pw1_b[0], od_dw_w[0], od_dw_b[0],
                   od_ln_g[0], od_ln_b[0], od_pw2_w[0], od_pw2_b[0])
    h = _cross_attn(h, xa_norm[1], xa_wq[1], kv[1], xa_wo[1])
    out = _moe_block(h.reshape(b * s, d), od_norm_moe[0], od_router[0], od_moe_gu[0],
                     od_moe_down[0], final_norm)
    return out.reshape(b, s, d)
```

```python
import functools

import jax
import jax.numpy as jnp
from jax import lax
from jax.experimental import pallas as pl
from jax.experimental.pallas import tpu as pltpu

EPS = 1e-6
BF16 = jnp.bfloat16
F32 = jnp.float32

POOL_WINDOWS = (2, 4, 8, 16)
XATTN_HEADS = 4
N_EXPERTS = 8
LANES = 128
SUBLANES = 8
VMEM_LIMIT = 56 * 1024 * 1024

SEQ_TILE = 512
MIX_HALO = 8
CONF_HALO = 16
CONV_ROWS = 32
FFN_CHUNK = 256
MOE_ROWS = 512
MOE_CHUNK = 1792
MOE_STEPS = 2
ROW_UNROLL = 4
TOK_CHUNK = 256


def _rms(x, g):
    return x * lax.rsqrt(jnp.mean(x * x, axis=-1, keepdims=True) + EPS) * g


def _dot(a, b):
    return jnp.dot(a, b, preferred_element_type=F32)


def _cparams(*sem):
    return pltpu.CompilerParams(dimension_semantics=sem, vmem_limit_bytes=VMEM_LIMIT)


def _const_spec(shape):
    nd = len(shape)
    return pl.BlockSpec(shape, lambda *_: (0,) * nd, pipeline_mode=pl.Buffered(1))


class _CastRider:
    def __init__(self, w, n_steps, col_blocks=1, col_block=0):
        e, r, c = w.shape
        per_expert = n_steps // e
        self.w = w
        self.block = (1, r // per_expert, c // col_blocks)
        self.out_shape = jax.ShapeDtypeStruct((e, r, c // col_blocks), BF16)
        self._per_expert = per_expert
        self._col_block = col_block

    def spec(self, step_of):
        pe, cb = self._per_expert, self._col_block

        def in_map(*idx):
            s = step_of(*idx)
            return (s // pe, s % pe, cb)

        def out_map(*idx):
            s = step_of(*idx)
            return (s // pe, s % pe, 0)

        return pl.BlockSpec(self.block, in_map), pl.BlockSpec(self.block, out_map)


def _cast_block(src_ref, dst_ref):
    dst_ref[...] = src_ref[...].astype(BF16)


def _even_mixer_kernel(x_ref, xp_ref, xn_ref, g_ref, win_ref, conv_ref, pw_ref,
                       ps_ref, wout_ref, cast_src_ref, o_ref, cast_dst_ref, z_ref, a_ref, *,
                       seq_len):
    _cast_block(cast_src_ref, cast_dst_ref)
    i = pl.program_id(1)
    n_i = pl.num_programs(1)
    ts = x_ref.shape[1]
    h0 = MIX_HALO
    aw = conv_ref.shape[1]
    bg = pw_ref.shape[1]
    xm = x_ref[0]
    xa = jnp.concatenate([xp_ref[0], xm, xn_ref[0]], axis=0)
    hn = _rms(xa, g_ref[...]).astype(BF16)
    z = _dot(hn, win_ref[...])
    row = lax.broadcasted_iota(jnp.int32, (ts + 2 * h0, 1), 0)
    inside = ((row >= h0) | (i > 0)) & ((row < ts + h0) | (i < n_i - 1))
    z_ref[...] = jnp.where(inside, z, 0.0)

    a_ref[...] = z_ref[:, 2 * aw:3 * aw] * z_ref[:, 0:aw]
    cw = conv_ref[...]
    conv = (cw[0:1] * a_ref[h0 - 1:h0 - 1 + ts, :] + cw[1:2] * a_ref[h0:h0 + ts, :]
            + cw[2:3] * a_ref[h0 + 1:h0 + 1 + ts, :])
    parts = [(z_ref[h0:h0 + ts, aw:2 * aw] * conv).astype(BF16)]

    pos = i * ts + lax.broadcasted_iota(jnp.int32, (ts, 1), 0)
    for gi, win in enumerate(POOL_WINDOWS):
        left = win // 2
        right = win - 1 - left
        c0 = 3 * aw + gi * bg
        s = z_ref[h0 - left:h0 - left + ts, c0:c0 + bg]
        for j in range(-left + 1, right + 1):
            s = s + z_ref[h0 + j:h0 + j + ts, c0:c0 + bg]
        cnt = (jnp.minimum(pos + right, seq_len - 1) - jnp.maximum(pos - left, 0) + 1).astype(F32)
        p = s / cnt - z_ref[h0:h0 + ts, c0:c0 + bg]
        yb = _dot(p.astype(BF16), pw_ref[gi]) * ps_ref[:, gi * bg:(gi + 1) * bg]
        parts.append(yb.astype(BF16))
    y = jnp.concatenate(parts, axis=1)
    o_ref[0] = xm + _dot(y, wout_ref[...])


def _even_mixer(x, g, w_in, conv_a, pool_w, pool_scale, w_out, rider):
    b, s, d = x.shape
    ts, h0 = SEQ_TILE, MIX_HALO
    rider_in, rider_out = rider.spec(lambda bi, i: bi * (s // ts) + i)
    r = ts // h0
    incols = w_in.shape[1]
    aw = conv_a.shape[1]
    return pl.pallas_call(
        functools.partial(_even_mixer_kernel, seq_len=s),
        grid=(b, s // ts),
        in_specs=[
            pl.BlockSpec((1, ts, d), lambda bi, i: (bi, i, 0)),
            pl.BlockSpec((1, h0, d), lambda bi, i: (bi, jnp.maximum(i * r - 1, 0), 0)),
            pl.BlockSpec((1, h0, d), lambda bi, i: (bi, jnp.minimum((i + 1) * r, s // h0 - 1), 0)),
            _const_spec((1, d)),
            _const_spec((d, incols)),
            _const_spec(conv_a.shape),
            _const_spec(pool_w.shape),
            _const_spec((1, pool_scale.shape[-1])),
            _const_spec(w_out.shape),
            rider_in,
        ],
        out_specs=[pl.BlockSpec((1, ts, d), lambda bi, i: (bi, i, 0)), rider_out],
        out_shape=[jax.ShapeDtypeStruct((b, s, d), F32), rider.out_shape],
        scratch_shapes=[pltpu.VMEM((ts + 2 * h0, incols), F32),
                        pltpu.VMEM((ts + 2 * h0, aw), F32)],
        compiler_params=_cparams("parallel", "parallel"),
        name="even_mixer",
    )(x, x, x, g.reshape(1, d), w_in.astype(BF16), conv_a, pool_w.astype(BF16),
      pool_scale.reshape(1, -1), w_out.astype(BF16), rider.w)


def _conformer_kernel(x_ref, xp_ref, xn_ref, g_ref, pw1_ref, b1_ref, dw_ref, dwb_ref,
                      lng_ref, lnb_ref, pw2_ref, b2_ref, cast_src_ref, o_ref, cast_dst_ref,
                      glu_ref, conv_ref):
    _cast_block(cast_src_ref, cast_dst_ref)
    i = pl.program_id(1)
    n_i = pl.num_programs(1)
    ts = x_ref.shape[1]
    d = x_ref.shape[2]
    h0 = CONF_HALO
    taps = dw_ref.shape[0] // SUBLANES
    half = taps // 2
    xm = x_ref[0]
    xa = jnp.concatenate([xp_ref[0], xm, xn_ref[0]], axis=0)
    hn = _rms(xa, g_ref[...]).astype(BF16)
    y1 = _dot(hn, pw1_ref[...]) + b1_ref[...]
    glu = y1[:, :d] * jax.nn.sigmoid(y1[:, d:])
    row = lax.broadcasted_iota(jnp.int32, (ts + 2 * h0, 1), 0)
    inside = ((row >= h0) | (i > 0)) & ((row < ts + h0) | (i < n_i - 1))
    glu_ref[0] = jnp.where(inside, glu, 0.0)
    span = ts + 2 * h0 - SUBLANES
    for r in range(1, SUBLANES):
        glu_ref[r, 0:span, :] = glu_ref[0, r:r + span, :]

    n_sub = CONV_ROWS // SUBLANES
    for c in range(d // LANES):
        cols = slice(c * LANES, (c + 1) * LANES)
        w_taps = [dw_ref[SUBLANES * k:SUBLANES * (k + 1), cols] for k in range(taps)]
        bias = dwb_ref[:, cols]

        def conv_block(rb, carry, cols=cols, w_taps=w_taps, bias=bias):
            base = pl.multiple_of(rb * CONV_ROWS, CONV_ROWS)
            accs = [jnp.zeros((SUBLANES, LANES), F32) + bias for _ in range(n_sub)]
            for k in range(taps):
                off = h0 - half + k
                start = base + (off // SUBLANES) * SUBLANES
                for a in range(n_sub):
                    accs[a] = accs[a] + w_taps[k] * glu_ref[off % SUBLANES,
                                                            pl.ds(start + a * SUBLANES, SUBLANES), cols]
            conv_ref[pl.ds(base, CONV_ROWS), cols] = jnp.concatenate(accs, axis=0)
            return carry

        lax.fori_loop(0, ts // CONV_ROWS, conv_block, 0)

    acc = conv_ref[...]
    mu = jnp.mean(acc, axis=-1, keepdims=True)
    xc = acc - mu
    var = jnp.mean(xc * xc, axis=-1, keepdims=True)
    yn = xc * lax.rsqrt(var + EPS) * lng_ref[...] + lnb_ref[...]
    act = (yn * jax.nn.sigmoid(yn)).astype(BF16)
    o_ref[0] = xm + _dot(act, pw2_ref[...]) + b2_ref[...]


def _conformer(x, g, pw1_w, pw1_b, dw_w, dw_b, ln_g, ln_b, pw2_w, pw2_b, rider):
    b, s, d = x.shape
    ts, h0 = SEQ_TILE, CONF_HALO
    rider_in, rider_out = rider.spec(lambda bi, i: bi * (s // ts) + i)
    r = ts // h0
    return pl.pallas_call(
        _conformer_kernel,
        grid=(b, s // ts),
        in_specs=[
            pl.BlockSpec((1, ts, d), lambda bi, i: (bi, i, 0)),
            pl.BlockSpec((1, h0, d), lambda bi, i: (bi, jnp.maximum(i * r - 1, 0), 0)),
            pl.BlockSpec((1, h0, d), lambda bi, i: (bi, jnp.minimum((i + 1) * r, s // h0 - 1), 0)),
            _const_spec((1, d)),
            _const_spec(pw1_w.shape),
            _const_spec((1, 2 * d)),
            _const_spec((dw_w.shape[0] * SUBLANES, d)),
            _const_spec((1, d)),
            _const_spec((1, d)),
            _const_spec((1, d)),
            _const_spec(pw2_w.shape),
            _const_spec((1, d)),
            rider_in,
        ],
        out_specs=[pl.BlockSpec((1, ts, d), lambda bi, i: (bi, i, 0)), rider_out],
        out_shape=[jax.ShapeDtypeStruct((b, s, d), F32), rider.out_shape],
        scratch_shapes=[pltpu.VMEM((SUBLANES, ts + 2 * h0, d), F32), pltpu.VMEM((ts, d), F32)],
        compiler_params=_cparams("parallel", "parallel"),
        name="conformer",
    )(x, x, x, g.reshape(1, d), pw1_w.astype(BF16), pw1_b.reshape(1, -1),
      jnp.repeat(dw_w, SUBLANES, axis=0),
      dw_b.reshape(1, d), ln_g.reshape(1, d), ln_b.reshape(1, d), pw2_w.astype(BF16),
      pw2_b.reshape(1, d), rider.w)


def _kv_kernel(mem_ref, g_ref, wkv_ref, kv_ref):
    mn = _rms(mem_ref[0], g_ref[0]).astype(BF16)
    kv_ref[0, 0] = _dot(mn, wkv_ref[0]).astype(BF16)


def _memory_kv(mem, norm_mem, wkv):
    b, m, d = mem.shape
    nl = wkv.shape[0]
    return pl.pallas_call(
        _kv_kernel,
        grid=(nl, b),
        in_specs=[
            pl.BlockSpec((1, m, d), lambda l, bi: (bi, 0, 0)),
            pl.BlockSpec((1, 1, d), lambda l, bi: (l, 0, 0)),
            pl.BlockSpec((1, d, 2 * d), lambda l, bi: (l, 0, 0)),
        ],
        out_specs=pl.BlockSpec((1, 1, m, 2 * d), lambda l, bi: (l, bi, 0, 0)),
        out_shape=jax.ShapeDtypeStruct((nl, b, m, 2 * d), BF16),
        compiler_params=_cparams("parallel", "parallel"),
        name="memory_kv",
    )(mem, norm_mem.reshape(nl, 1, d), wkv.astype(BF16))


def _xattn_kernel(h_ref, g_ref, wq_ref, kv_ref, wo_ref, o_ref):
    d = h_ref.shape[2]
    hd = d // XATTN_HEADS
    hb = h_ref[0]
    hn = _rms(hb, g_ref[...]).astype(BF16)
    q = (_dot(hn, wq_ref[...]) * (hd ** -0.5)).astype(BF16)
    kv = kv_ref[0]
    outs = []
    for a in range(XATTN_HEADS):
        qh = q[:, a * hd:(a + 1) * hd]
        kh = kv[:, a * hd:(a + 1) * hd]
        vh = kv[:, d + a * hd:d + (a + 1) * hd]
        sc = lax.dot_general(qh, kh, (((1,), (1,)), ((), ())), preferred_element_type=F32)
        e = jnp.exp(sc - jnp.max(sc, axis=-1, keepdims=True))
        o = _dot(e.astype(BF16), vh) / jnp.sum(e, axis=-1, keepdims=True)
        outs.append(o.astype(BF16))
    o_ref[0] = hb + _dot(jnp.concatenate(outs, axis=1), wo_ref[...])


def _cross_attn(h, g, wq, kv, wo):
    b, s, d = h.shape
    ts = SEQ_TILE
    m = kv.shape[1]
    return pl.pallas_call(
        _xattn_kernel,
        grid=(b, s // ts),
        in_specs=[
            pl.BlockSpec((1, ts, d), lambda bi, i: (bi, i, 0)),
            _const_spec((1, d)),
            _const_spec((d, d)),
            pl.BlockSpec((1, m, 2 * d), lambda bi, i: (bi, 0, 0)),
            _const_spec((d, d)),
        ],
        out_specs=pl.BlockSpec((1, ts, d), lambda bi, i: (bi, i, 0)),
        out_shape=jax.ShapeDtypeStruct((b, s, d), F32),
        compiler_params=_cparams("parallel", "parallel"),
        name="cross_attn",
    )(h, g.reshape(1, d), wq.astype(BF16), kv, wo.astype(BF16))


def _swiglu_kernel(h_ref, g_ref, wgu_ref, wd_ref, cast_src_ref, o_ref, cast_dst_ref):
    _cast_block(cast_src_ref, cast_dst_ref)
    ff = wd_ref.shape[0]
    hb = h_ref[...]
    hn = _rms(hb, g_ref[...]).astype(BF16)
    acc = hb
    for c in range(ff // FFN_CHUNK):
        lo = c * FFN_CHUNK
        gate = _dot(hn, wgu_ref[:, lo:lo + FFN_CHUNK])
        up = _dot(hn, wgu_ref[:, ff + lo:ff + lo + FFN_CHUNK])
        act = (gate * jax.nn.sigmoid(gate) * up).astype(BF16)
        acc = acc + _dot(act, wd_ref[lo:lo + FFN_CHUNK, :])
    o_ref[...] = acc


def _dense_swiglu(h2, g, w_gu, w_down, rider):
    t, d = h2.shape
    ts = SEQ_TILE
    rider_in, rider_out = rider.spec(lambda i: i)
    return pl.pallas_call(
        _swiglu_kernel,
        grid=(t // ts,),
        in_specs=[
            pl.BlockSpec((ts, d), lambda i: (i, 0)),
            _const_spec((1, d)),
            _const_spec(w_gu.shape),
            _const_spec(w_down.shape),
            rider_in,
        ],
        out_specs=[pl.BlockSpec((ts, d), lambda i: (i, 0)), rider_out],
        out_shape=[jax.ShapeDtypeStruct((t, d), F32), rider.out_shape],
        compiler_params=_cparams("parallel"),
        name="dense_swiglu",
    )(h2, g.reshape(1, d), w_gu.astype(BF16), w_down.astype(BF16), rider.w)


def _store_row_tiles(ref, x):
    for s in range(SUBLANES):
        ref[pl.ds(s, x.shape[0], stride=SUBLANES), :] = x[:, s * LANES:(s + 1) * LANES]


def _load_row_tiles(ref):
    rows = ref.shape[0] // SUBLANES
    return jnp.concatenate([ref[pl.ds(s, rows, stride=SUBLANES), :] for s in range(SUBLANES)], axis=1)


def _row_tile(ref, r):
    return ref.at[pl.ds(pl.multiple_of(r * SUBLANES, SUBLANES), SUBLANES), :]


def _router_kernel(h_ref, g_ref, wr_ref, hn_ref, meta_ref, total_ref, carry_ref):
    i = pl.program_id(0)
    tc = h_ref.shape[0]

    @pl.when(i == 0)
    def _():
        carry_ref[...] = jnp.zeros_like(carry_ref)

    @pl.when(i == pl.num_programs(0) - 1)
    def _():
        hn_ref[...] = jnp.zeros_like(hn_ref)

    @pl.when(i < pl.num_programs(0) - 1)
    def _():
        _route_chunk(h_ref, g_ref, wr_ref, hn_ref, meta_ref, total_ref, carry_ref)


def _route_chunk(h_ref, g_ref, wr_ref, hn_ref, meta_ref, total_ref, carry_ref):
    tc = h_ref.shape[0]
    hn = _rms(h_ref[...], g_ref[...])
    _store_row_tiles(hn_ref, hn)
    logits = jnp.dot(hn, wr_ref[...], preferred_element_type=F32,
                     precision=lax.Precision.HIGHEST)
    lane = lax.broadcasted_iota(jnp.int32, (tc, LANES), 1)
    neg = jnp.float32(-jnp.inf)
    l1 = jnp.where(lane < N_EXPERTS, logits, neg)
    m1 = jnp.max(l1, axis=-1, keepdims=True)
    i1 = jnp.min(jnp.where(l1 == m1, lane, LANES), axis=-1, keepdims=True)
    l2 = jnp.where(lane == i1, neg, l1)
    m2 = jnp.max(l2, axis=-1, keepdims=True)
    i2 = jnp.min(jnp.where(l2 == m2, lane, LANES), axis=-1, keepdims=True)
    ex = jnp.exp(m2 - m1)
    w1 = 1.0 / (1.0 + ex)
    w2 = ex / (1.0 + ex)
    sel1 = lane == i1
    sel2 = lane == i2
    onehot = (sel1 | sel2).astype(BF16)
    r = lax.broadcasted_iota(jnp.int32, (tc, tc), 0)
    c = lax.broadcasted_iota(jnp.int32, (tc, tc), 1)
    lower = (c < r).astype(BF16)
    carry = carry_ref[...]
    prior = _dot(lower, onehot) + carry
    rank1 = jnp.sum(jnp.where(sel1, prior, 0.0), axis=-1, keepdims=True)
    rank2 = jnp.sum(jnp.where(sel2, prior, 0.0), axis=-1, keepdims=True)
    meta = jnp.where(lane == 0, i1.astype(F32), 0.0)
    meta = jnp.where(lane == 1, i2.astype(F32), meta)
    meta = jnp.where(lane == 2, rank1, meta)
    meta = jnp.where(lane == 3, rank2, meta)
    meta = jnp.where(lane == 4, w1, meta)
    meta = jnp.where(lane == 5, w2, meta)
    meta_ref[...] = meta
    new_carry = carry + jnp.sum(onehot.astype(F32), axis=0, keepdims=True)
    carry_ref[...] = new_carry
    total_ref[...] = new_carry


def _router(h2, g, router_w):
    t, d = h2.shape
    tc = TOK_CHUNK
    nc = t // tc
    wr = jnp.zeros((d, LANES), F32).at[:, :router_w.shape[1]].set(router_w)
    last = nc - 1
    return pl.pallas_call(
        _router_kernel,
        grid=(nc + 1,),
        in_specs=[
            pl.BlockSpec((tc, d), lambda i: (jnp.minimum(i, last), 0)),
            _const_spec((1, d)),
            _const_spec((d, LANES)),
        ],
        out_specs=[
            pl.BlockSpec((tc * SUBLANES, LANES), lambda i: (i, 0)),
            pl.BlockSpec((tc, LANES), lambda i: (jnp.minimum(i, last), 0)),
            pl.BlockSpec((1, LANES), lambda i: (0, 0)),
        ],
        out_shape=[
            jax.ShapeDtypeStruct(((t + tc) * SUBLANES, LANES), F32),
            jax.ShapeDtypeStruct((t, LANES), F32),
            jax.ShapeDtypeStruct((1, LANES), F32),
        ],
        scratch_shapes=[pltpu.VMEM((1, LANES), F32)],
        compiler_params=_cparams("arbitrary"),
        name="moe_router",
    )(h2, g.reshape(1, d), wr)


def _row_copy(src, dst, sem):
    return pltpu.make_async_copy(src, dst, sem)


def _experts_kernel(te_ref, nt_ref, nreal_ref, inv_cur_ref, inv_nxt_ref, inv_prv_ref, hn_hbm,
                    wg_ref, wu_ref, wd_ref, y_hbm, xin_ref, x_ref, acc_ref, yout_ref,
                    gsem, ssem, *, n_tokens):
    q = pl.program_id(0)
    f = pl.program_id(1)
    nf = pl.num_programs(1)
    nt = nt_ref[0]
    rows = x_ref.shape[0]
    live = q < nt
    slot = q % 2
    other = 1 - slot
    share = -(-rows // MOE_STEPS)
    n_prev = jnp.where(q >= 1, nreal_ref[jnp.maximum(q - 1, 0)], 0)
    n_prev2 = nreal_ref[jnp.maximum(q - 2, 0)]

    def gather_one(idx_ref, dst_slot, j):
        a = idx_ref[0, 0, j]
        src = jnp.where(a < 0, n_tokens, jnp.where(a >= n_tokens, a - n_tokens, a))
        _row_copy(_row_tile(hn_hbm, src), _row_tile(xin_ref.at[dst_slot], j), gsem.at[dst_slot]).start()

    def scatter_one(src_slot, j):
        _row_copy(_row_tile(yout_ref.at[src_slot], j), _row_tile(y_hbm, inv_prv_ref[0, 0, j]),
                  ssem.at[src_slot]).start()

    def wait_gathered(s):
        _row_copy(hn_hbm.at[pl.ds(0, rows * SUBLANES), :], xin_ref.at[s], gsem.at[s]).wait()

    def wait_scattered(s, n):
        size = pl.multiple_of(n * SUBLANES, SUBLANES)
        _row_copy(yout_ref.at[s, pl.ds(0, size), :], y_hbm.at[pl.ds(0, size), :], ssem.at[s]).wait()

    @pl.when((q == 0) & (f == 0))
    def _():
        def one(j, c):
            gather_one(inv_cur_ref, 0, j)
            return c
        lax.fori_loop(0, rows, one, 0, unroll=ROW_UNROLL)

    @pl.when((q >= 2) & (q <= nt) & (f == 0))
    def _():
        wait_scattered(slot, n_prev2)

    @pl.when((q <= nt) & (f == 0))
    def _():
        wait_gathered(slot)

    @pl.when(live & (f == 0))
    def _():
        x_ref[...] = _load_row_tiles(xin_ref.at[slot]).astype(BF16)
        acc_ref[...] = jnp.zeros_like(acc_ref)

    @pl.when(live)
    def _():
        for i in range(share):
            j = f * share + i
            jc = jnp.minimum(j, rows - 1)

            @pl.when(j < rows)
            def _():
                gather_one(inv_nxt_ref, other, jc)

            @pl.when(j < n_prev)
            def _():
                scatter_one(other, jc)

        x = x_ref[...]
        gate = _dot(x, wg_ref[0])
        up = _dot(x, wu_ref[0])
        act = (gate * jax.nn.sigmoid(gate) * up).astype(BF16)
        acc_ref[...] += _dot(act, wd_ref[0])

    @pl.when(live & (f == nf - 1))
    def _():
        _store_row_tiles(yout_ref.at[slot], acc_ref[...])

    @pl.when((q == nt) & (f == 0))
    def _():
        def one(j, c):
            scatter_one(other, j)
            return c
        lax.fori_loop(0, n_prev, one, 0)
        wait_scattered(other, n_prev)


def _experts(hn_rows, inv, tile_expert, n_tiles, tile_real, w_gate, w_up, w_down, n_tokens):
    d = SUBLANES * LANES
    ne, ff, _ = w_down.shape
    assert w_down.shape[2] == d
    nf = ff // MOE_CHUNK
    assert nf == MOE_STEPS
    nq = inv.shape[0]

    def tile(q, nt):
        return jnp.minimum(q, nt[0] - 1)

    def chunk(q, f, nt):
        return jnp.where(q < nt[0], f, nf - 1)

    idx_block = (1, 1, MOE_ROWS)
    return pl.pallas_call(
        functools.partial(_experts_kernel, n_tokens=n_tokens),
        grid_spec=pltpu.PrefetchScalarGridSpec(
            num_scalar_prefetch=3,
            grid=(nq + 1, nf),
            in_specs=[
                pl.BlockSpec(idx_block, lambda q, f, te, nt, nr: (jnp.minimum(q, nq - 1), 0, 0),
                             memory_space=pltpu.SMEM),
                pl.BlockSpec(idx_block, lambda q, f, te, nt, nr: (jnp.minimum(q + 1, nq - 1), 0, 0),
                             memory_space=pltpu.SMEM),
                pl.BlockSpec(idx_block, lambda q, f, te, nt, nr: (jnp.clip(q - 1, 0, nq - 1), 0, 0),
                             memory_space=pltpu.SMEM),
                pl.BlockSpec(memory_space=pl.ANY),
                pl.BlockSpec((1, d, MOE_CHUNK),
                             lambda q, f, te, nt, nr: (te[tile(q, nt)], 0, chunk(q, f, nt))),
                pl.BlockSpec((1, d, MOE_CHUNK),
                             lambda q, f, te, nt, nr: (te[tile(q, nt)], 0, chunk(q, f, nt))),
                pl.BlockSpec((1, MOE_CHUNK, d),
                             lambda q, f, te, nt, nr: (te[tile(q, nt)], chunk(q, f, nt), 0)),
            ],
            out_specs=pl.BlockSpec(memory_space=pl.ANY),
            scratch_shapes=[
                pltpu.VMEM((2, MOE_ROWS * SUBLANES, LANES), F32),
                pltpu.VMEM((MOE_ROWS, d), BF16),
                pltpu.VMEM((MOE_ROWS, d), F32),
                pltpu.VMEM((2, MOE_ROWS * SUBLANES, LANES), F32),
                pltpu.SemaphoreType.DMA((2,)),
                pltpu.SemaphoreType.DMA((2,)),
            ],
        ),
        out_shape=jax.ShapeDtypeStruct((2 * n_tokens * SUBLANES, LANES), F32),
        compiler_params=pltpu.CompilerParams(
            dimension_semantics=("arbitrary", "arbitrary"), vmem_limit_bytes=VMEM_LIMIT,
            disable_bounds_checks=True),
        name="moe_experts",
    )(tile_expert, n_tiles, tile_real, inv, inv, inv, hn_rows, w_gate, w_up, w_down)


def _combine_kernel(h_ref, y1_ref, y2_ref, wts_ref, g_ref, o_ref):
    wt = wts_ref[...]
    moe = wt[:, 0:1] * _load_row_tiles(y1_ref) + wt[:, 1:2] * _load_row_tiles(y2_ref)
    o_ref[...] = _rms(h_ref[...] + moe, g_ref[...])


def _combine(h2, y, wts, g_final):
    t, d = h2.shape
    ts = SEQ_TILE
    nt = t // ts
    return pl.pallas_call(
        _combine_kernel,
        grid=(nt,),
        in_specs=[
            pl.BlockSpec((ts, d), lambda i: (i, 0)),
            pl.BlockSpec((ts * SUBLANES, LANES), lambda i: (i, 0)),
            pl.BlockSpec((ts * SUBLANES, LANES), lambda i: (nt + i, 0)),
            pl.BlockSpec((ts, 2), lambda i: (i, 0)),
            _const_spec((1, d)),
        ],
        out_specs=pl.BlockSpec((ts, d), lambda i: (i, 0)),
        out_shape=jax.ShapeDtypeStruct((t, d), F32),
        compiler_params=_cparams("parallel"),
        name="moe_combine",
    )(h2, y, y, wts, g_final.reshape(1, d))


def _moe_block(h2, g, router_w, w_gate, w_up, w_down, g_final):
    t, d = h2.shape
    ne = w_down.shape[0]
    nq = (2 * t) // MOE_ROWS + ne
    n_rows = nq * MOE_ROWS

    hn_rows, meta, total = _router(h2, g, router_w)

    experts = meta[:, 0:2].astype(jnp.int32)
    ranks = meta[:, 2:4].astype(jnp.int32)
    wts = meta[:, 4:6]
    counts = total[0, :ne].astype(jnp.int32)
    tiles_per = (counts + MOE_ROWS - 1) // MOE_ROWS
    tile_end = jnp.cumsum(tiles_per)
    tile_start = tile_end - tiles_per
    n_tiles = tile_end[-1]
    dest = (jnp.take(tile_start * MOE_ROWS, experts) + ranks).T.reshape(-1)
    qs = jnp.arange(nq, dtype=jnp.int32)
    tile_expert = jnp.minimum(jnp.sum((qs[:, None] >= tile_end[None, :]).astype(jnp.int32), axis=1),
                              ne - 1)
    tile_real = jnp.clip(jnp.take(counts, tile_expert) - (qs - jnp.take(tile_start, tile_expert)) * MOE_ROWS,
                         0, MOE_ROWS)
    inv = jnp.full((n_rows,), -1, jnp.int32).at[dest].set(jnp.arange(2 * t, dtype=jnp.int32))
    y = _experts(hn_rows, inv.reshape(nq, 1, MOE_ROWS), tile_expert, n_tiles.reshape(1), tile_real,
                 w_gate, w_up, w_down, t)
    return _combine(h2, y, wts, g_final)


def kernel(x, mem, ev_norm_mix, ev_w_in, ev_conv_a, ev_pool_w, ev_pool_scale, ev_w_out, ev_norm_ffn, ev_ffn_gu, ev_ffn_down, od_norm_mix, od_pw1_w, od_pw1_b, od_dw_w, od_dw_b, od_ln_g, od_ln_b, od_pw2_w, od_pw2_b, od_norm_moe, od_router, od_moe_gu, od_moe_down, xa_norm, xa_norm_mem, xa_wq, xa_wkv, xa_wo, final_norm):
    b, s, d = x.shape
    assert ev_w_in.shape[0] == 1 and od_pw1_w.shape[0] == 1 and xa_wq.shape[0] == 2
    kv = _memory_kv(mem, xa_norm_mem, xa_wkv)
    n_steps = (b * s) // SEQ_TILE
    h, moe_down = _even_mixer(x, ev_norm_mix[0], ev_w_in[0], ev_conv_a[0], ev_pool_w[0],
                              ev_pool_scale[0], ev_w_out[0], _CastRider(od_moe_down[0], n_steps))
    h = _cross_attn(h, xa_norm[0], xa_wq[0], kv[0], xa_wo[0])
    h, moe_gate = _dense_swiglu(h.reshape(b * s, d), ev_norm_ffn[0], ev_ffn_gu[0], ev_ffn_down[0],
                                _CastRider(od_moe_gu[0], n_steps, col_blocks=2, col_block=0))
    h, moe_up = _conformer(h.reshape(b, s, d), od_norm_mix[0], od_pw1_w[0], od_pw1_b[0], od_dw_w[0],
                           od_dw_b[0], od_ln_g[0], od_ln_b[0], od_pw2_w[0], od_pw2_b[0],
                           _CastRider(od_moe_gu[0], n_steps, col_blocks=2, col_block=1))
    h = _cross_attn(h, xa_norm[1], xa_wq[1], kv[1], xa_wo[1])
    out = _moe_block(h.reshape(b * s, d), od_norm_moe[0], od_router[0], moe_gate, moe_up,
                     moe_down, final_norm)
    return out.reshape(b, s, d)
```

```python
import functools

import jax
import jax.numpy as jnp
from jax import lax
from jax.experimental import pallas as pl
from jax.experimental.pallas import tpu as pltpu

EPS = 1e-6
BF16 = jnp.bfloat16
F32 = jnp.float32

POOL_WINDOWS = (2, 4, 8, 16)
XATTN_HEADS = 4
N_EXPERTS = 8
LANES = 128
SUBLANES = 8
VMEM_LIMIT = 56 * 1024 * 1024

SEQ_TILE = 512
MIX_HALO = 8
CONF_HALO = 16
CONV_ROWS = 32
FFN_CHUNK = 256
MOE_ROWS = 512
MOE_CHUNK = 1792
MOE_STEPS = 2
ROW_BATCH = 8
TOK_CHUNK = 512


def _rms(x, g):
    return x * lax.rsqrt(jnp.mean(x * x, axis=-1, keepdims=True) + EPS) * g


def _dot(a, b):
    return jnp.dot(a, b, preferred_element_type=F32)


def _cparams(*sem):
    return pltpu.CompilerParams(dimension_semantics=sem, vmem_limit_bytes=VMEM_LIMIT)


def _const_spec(shape):
    nd = len(shape)
    return pl.BlockSpec(shape, lambda *_: (0,) * nd, pipeline_mode=pl.Buffered(1))


class _CastRider:
    def __init__(self, w, n_steps, col_blocks=1, col_block=0):
        e, r, c = w.shape
        per_expert = n_steps // e
        self.w = w
        self.block = (1, r // per_expert, c // col_blocks)
        self.out_shape = jax.ShapeDtypeStruct((e, r, c // col_blocks), BF16)
        self._per_expert = per_expert
        self._col_block = col_block

    def spec(self, step_of):
        pe, cb = self._per_expert, self._col_block

        def in_map(*idx):
            s = step_of(*idx)
            return (s // pe, s % pe, cb)

        def out_map(*idx):
            s = step_of(*idx)
            return (s // pe, s % pe, 0)

        return pl.BlockSpec(self.block, in_map), pl.BlockSpec(self.block, out_map)


def _cast_block(src_ref, dst_ref):
    dst_ref[...] = src_ref[...].astype(BF16)


def _even_mixer_kernel(x_ref, xp_ref, xn_ref, g_ref, win_ref, conv_ref, pw_ref,
                       ps_ref, wout_ref, cast_src_ref, o_ref, cast_dst_ref, z_ref, a_ref, *,
                       seq_len):
    _cast_block(cast_src_ref, cast_dst_ref)
    i = pl.program_id(1)
    n_i = pl.num_programs(1)
    ts = x_ref.shape[1]
    h0 = MIX_HALO
    aw = conv_ref.shape[1]
    bg = pw_ref.shape[1]
    xm = x_ref[0]
    xa = jnp.concatenate([xp_ref[0], xm, xn_ref[0]], axis=0)
    hn = _rms(xa, g_ref[...]).astype(BF16)
    z = _dot(hn, win_ref[...])
    row = lax.broadcasted_iota(jnp.int32, (ts + 2 * h0, 1), 0)
    inside = ((row >= h0) | (i > 0)) & ((row < ts + h0) | (i < n_i - 1))
    z_ref[...] = jnp.where(inside, z, 0.0)

    a_ref[...] = z_ref[:, 2 * aw:3 * aw] * z_ref[:, 0:aw]
    cw = conv_ref[...]
    conv = (cw[0:1] * a_ref[h0 - 1:h0 - 1 + ts, :] + cw[1:2] * a_ref[h0:h0 + ts, :]
            + cw[2:3] * a_ref[h0 + 1:h0 + 1 + ts, :])
    parts = [(z_ref[h0:h0 + ts, aw:2 * aw] * conv).astype(BF16)]

    pos = i * ts + lax.broadcasted_iota(jnp.int32, (ts, 1), 0)
    for gi, win in enumerate(POOL_WINDOWS):
        left = win // 2
        right = win - 1 - left
        c0 = 3 * aw + gi * bg
        s = z_ref[h0 - left:h0 - left + ts, c0:c0 + bg]
        for j in range(-left + 1, right + 1):
            s = s + z_ref[h0 + j:h0 + j + ts, c0:c0 + bg]
        cnt = (jnp.minimum(pos + right, seq_len - 1) - jnp.maximum(pos - left, 0) + 1).astype(F32)
        p = s / cnt - z_ref[h0:h0 + ts, c0:c0 + bg]
        yb = _dot(p.astype(BF16), pw_ref[gi]) * ps_ref[:, gi * bg:(gi + 1) * bg]
        parts.append(yb.astype(BF16))
    y = jnp.concatenate(parts, axis=1)
    o_ref[0] = xm + _dot(y, wout_ref[...])


def _even_mixer(x, g, w_in, conv_a, pool_w, pool_scale, w_out, rider):
    b, s, d = x.shape
    ts, h0 = SEQ_TILE, MIX_HALO
    rider_in, rider_out = rider.spec(lambda bi, i: bi * (s // ts) + i)
    r = ts // h0
    incols = w_in.shape[1]
    aw = conv_a.shape[1]
    return pl.pallas_call(
        functools.partial(_even_mixer_kernel, seq_len=s),
        grid=(b, s // ts),
        in_specs=[
            pl.BlockSpec((1, ts, d), lambda bi, i: (bi, i, 0)),
            pl.BlockSpec((1, h0, d), lambda bi, i: (bi, jnp.maximum(i * r - 1, 0), 0)),
            pl.BlockSpec((1, h0, d), lambda bi, i: (bi, jnp.minimum((i + 1) * r, s // h0 - 1), 0)),
            _const_spec((1, d)),
            _const_spec((d, incols)),
            _const_spec(conv_a.shape),
            _const_spec(pool_w.shape),
            _const_spec((1, pool_scale.shape[-1])),
            _const_spec(w_out.shape),
            rider_in,
        ],
        out_specs=[pl.BlockSpec((1, ts, d), lambda bi, i: (bi, i, 0)), rider_out],
        out_shape=[jax.ShapeDtypeStruct((b, s, d), F32), rider.out_shape],
        scratch_shapes=[pltpu.VMEM((ts + 2 * h0, incols), F32),
                        pltpu.VMEM((ts + 2 * h0, aw), F32)],
        compiler_params=_cparams("parallel", "parallel"),
        name="even_mixer",
    )(x, x, x, g.reshape(1, d), w_in.astype(BF16), conv_a, pool_w.astype(BF16),
      pool_scale.reshape(1, -1), w_out.astype(BF16), rider.w)


def _conformer_kernel(x_ref, xp_ref, xn_ref, g_ref, pw1_ref, b1_ref, dw_ref, dwb_ref,
                      lng_ref, lnb_ref, pw2_ref, b2_ref, cast_src_ref, o_ref, cast_dst_ref,
                      glu_ref, conv_ref):
    _cast_block(cast_src_ref, cast_dst_ref)
    i = pl.program_id(1)
    n_i = pl.num_programs(1)
    ts = x_ref.shape[1]
    d = x_ref.shape[2]
    h0 = CONF_HALO
    taps = dw_ref.shape[0] // SUBLANES
    half = taps // 2
    xm = x_ref[0]
    xa = jnp.concatenate([xp_ref[0], xm, xn_ref[0]], axis=0)
    hn = _rms(xa, g_ref[...]).astype(BF16)
    y1 = _dot(hn, pw1_ref[...]) + b1_ref[...]
    glu = y1[:, :d] * jax.nn.sigmoid(y1[:, d:])
    row = lax.broadcasted_iota(jnp.int32, (ts + 2 * h0, 1), 0)
    inside = ((row >= h0) | (i > 0)) & ((row < ts + h0) | (i < n_i - 1))
    glu_ref[0] = jnp.where(inside, glu, 0.0)
    span = ts + 2 * h0 - SUBLANES
    for r in range(1, SUBLANES):
        glu_ref[r, 0:span, :] = glu_ref[0, r:r + span, :]

    n_sub = CONV_ROWS // SUBLANES
    for c in range(d // LANES):
        cols = slice(c * LANES, (c + 1) * LANES)
        w_taps = [dw_ref[SUBLANES * k:SUBLANES * (k + 1), cols] for k in range(taps)]
        bias = dwb_ref[:, cols]

        def conv_block(rb, carry, cols=cols, w_taps=w_taps, bias=bias):
            base = pl.multiple_of(rb * CONV_ROWS, CONV_ROWS)
            accs = [jnp.zeros((SUBLANES, LANES), F32) + bias for _ in range(n_sub)]
            for k in range(taps):
                off = h0 - half + k
                start = base + (off // SUBLANES) * SUBLANES
                for a in range(n_sub):
                    accs[a] = accs[a] + w_taps[k] * glu_ref[off % SUBLANES,
                                                            pl.ds(start + a * SUBLANES, SUBLANES), cols]
            conv_ref[pl.ds(base, CONV_ROWS), cols] = jnp.concatenate(accs, axis=0)
            return carry

        lax.fori_loop(0, ts // CONV_ROWS, conv_block, 0)

    acc = conv_ref[...]
    mu = jnp.mean(acc, axis=-1, keepdims=True)
    xc = acc - mu
    var = jnp.mean(xc * xc, axis=-1, keepdims=True)
    yn = xc * lax.rsqrt(var + EPS) * lng_ref[...] + lnb_ref[...]
    act = (yn * jax.nn.sigmoid(yn)).astype(BF16)
    o_ref[0] = xm + _dot(act, pw2_ref[...]) + b2_ref[...]


def _conformer(x, g, pw1_w, pw1_b, dw_w, dw_b, ln_g, ln_b, pw2_w, pw2_b, rider):
    b, s, d = x.shape
    ts, h0 = SEQ_TILE, CONF_HALO
    rider_in, rider_out = rider.spec(lambda bi, i: bi * (s // ts) + i)
    r = ts // h0
    return pl.pallas_call(
        _conformer_kernel,
        grid=(b, s // ts),
        in_specs=[
            pl.BlockSpec((1, ts, d), lambda bi, i: (bi, i, 0)),
            pl.BlockSpec((1, h0, d), lambda bi, i: (bi, jnp.maximum(i * r - 1, 0), 0)),
            pl.BlockSpec((1, h0, d), lambda bi, i: (bi, jnp.minimum((i + 1) * r, s // h0 - 1), 0)),
            _const_spec((1, d)),
            _const_spec(pw1_w.shape),
            _const_spec((1, 2 * d)),
            _const_spec((dw_w.shape[0] * SUBLANES, d)),
            _const_spec((1, d)),
            _const_spec((1, d)),
            _const_spec((1, d)),
            _const_spec(pw2_w.shape),
            _const_spec((1, d)),
            rider_in,
        ],
        out_specs=[pl.BlockSpec((1, ts, d), lambda bi, i: (bi, i, 0)), rider_out],
        out_shape=[jax.ShapeDtypeStruct((b, s, d), F32), rider.out_shape],
        scratch_shapes=[pltpu.VMEM((SUBLANES, ts + 2 * h0, d), F32), pltpu.VMEM((ts, d), F32)],
        compiler_params=_cparams("parallel", "parallel"),
        name="conformer",
    )(x, x, x, g.reshape(1, d), pw1_w.astype(BF16), pw1_b.reshape(1, -1),
      jnp.repeat(dw_w, SUBLANES, axis=0),
      dw_b.reshape(1, d), ln_g.reshape(1, d), ln_b.reshape(1, d), pw2_w.astype(BF16),
      pw2_b.reshape(1, d), rider.w)


def _kv_kernel(mem_ref, g_ref, wkv_ref, kv_ref):
    mn = _rms(mem_ref[0], g_ref[0]).astype(BF16)
    kv_ref[0, 0] = _dot(mn, wkv_ref[0]).astype(BF16)


def _memory_kv(mem, norm_mem, wkv):
    b, m, d = mem.shape
    nl = wkv.shape[0]
    return pl.pallas_call(
        _kv_kernel,
        grid=(nl, b),
        in_specs=[
            pl.BlockSpec((1, m, d), lambda l, bi: (bi, 0, 0)),
            pl.BlockSpec((1, 1, d), lambda l, bi: (l, 0, 0)),
            pl.BlockSpec((1, d, 2 * d), lambda l, bi: (l, 0, 0)),
        ],
        out_specs=pl.BlockSpec((1, 1, m, 2 * d), lambda l, bi: (l, bi, 0, 0)),
        out_shape=jax.ShapeDtypeStruct((nl, b, m, 2 * d), BF16),
        compiler_params=_cparams("parallel", "parallel"),
        name="memory_kv",
    )(mem, norm_mem.reshape(nl, 1, d), wkv.astype(BF16))


def _xattn_kernel(h_ref, g_ref, wq_ref, kv_ref, wo_ref, o_ref):
    d = h_ref.shape[2]
    hd = d // XATTN_HEADS
    hb = h_ref[0]
    hn = _rms(hb, g_ref[...]).astype(BF16)
    q = (_dot(hn, wq_ref[...]) * (hd ** -0.5)).astype(BF16)
    kv = kv_ref[0]
    outs = []
    for a in range(XATTN_HEADS):
        qh = q[:, a * hd:(a + 1) * hd]
        kh = kv[:, a * hd:(a + 1) * hd]
        vh = kv[:, d + a * hd:d + (a + 1) * hd]
        sc = lax.dot_general(qh, kh, (((1,), (1,)), ((), ())), preferred_element_type=F32)
        e = jnp.exp(sc - jnp.max(sc, axis=-1, keepdims=True))
        o = _dot(e.astype(BF16), vh) / jnp.sum(e, axis=-1, keepdims=True)
        outs.append(o.astype(BF16))
    o_ref[0] = hb + _dot(jnp.concatenate(outs, axis=1), wo_ref[...])


def _cross_attn(h, g, wq, kv, wo):
    b, s, d = h.shape
    ts = SEQ_TILE
    m = kv.shape[1]
    return pl.pallas_call(
        _xattn_kernel,
        grid=(b, s // ts),
        in_specs=[
            pl.BlockSpec((1, ts, d), lambda bi, i: (bi, i, 0)),
            _const_spec((1, d)),
            _const_spec((d, d)),
            pl.BlockSpec((1, m, 2 * d), lambda bi, i: (bi, 0, 0)),
            _const_spec((d, d)),
        ],
        out_specs=pl.BlockSpec((1, ts, d), lambda bi, i: (bi, i, 0)),
        out_shape=jax.ShapeDtypeStruct((b, s, d), F32),
        compiler_params=_cparams("parallel", "parallel"),
        name="cross_attn",
    )(h, g.reshape(1, d), wq.astype(BF16), kv, wo.astype(BF16))


def _swiglu_kernel(h_ref, g_ref, wgu_ref, wd_ref, cast_src_ref, o_ref, cast_dst_ref):
    _cast_block(cast_src_ref, cast_dst_ref)
    ff = wd_ref.shape[0]
    hb = h_ref[...]
    hn = _rms(hb, g_ref[...]).astype(BF16)
    acc = hb
    for c in range(ff // FFN_CHUNK):
        lo = c * FFN_CHUNK
        gate = _dot(hn, wgu_ref[:, lo:lo + FFN_CHUNK])
        up = _dot(hn, wgu_ref[:, ff + lo:ff + lo + FFN_CHUNK])
        act = (gate * jax.nn.sigmoid(gate) * up).astype(BF16)
        acc = acc + _dot(act, wd_ref[lo:lo + FFN_CHUNK, :])
    o_ref[...] = acc


def _dense_swiglu(h2, g, w_gu, w_down, rider):
    t, d = h2.shape
    ts = SEQ_TILE
    rider_in, rider_out = rider.spec(lambda i: i)
    return pl.pallas_call(
        _swiglu_kernel,
        grid=(t // ts,),
        in_specs=[
            pl.BlockSpec((ts, d), lambda i: (i, 0)),
            _const_spec((1, d)),
            _const_spec(w_gu.shape),
            _const_spec(w_down.shape),
            rider_in,
        ],
        out_specs=[pl.BlockSpec((ts, d), lambda i: (i, 0)), rider_out],
        out_shape=[jax.ShapeDtypeStruct((t, d), F32), rider.out_shape],
        compiler_params=_cparams("parallel"),
        name="dense_swiglu",
    )(h2, g.reshape(1, d), w_gu.astype(BF16), w_down.astype(BF16), rider.w)


def _store_row_tiles(ref, x):
    for s in range(SUBLANES):
        ref[pl.ds(s, x.shape[0], stride=SUBLANES), :] = x[:, s * LANES:(s + 1) * LANES]


def _load_row_tiles(ref):
    rows = ref.shape[0] // SUBLANES
    return jnp.concatenate([ref[pl.ds(s, rows, stride=SUBLANES), :] for s in range(SUBLANES)], axis=1)


def _row_tile(ref, r):
    return ref.at[pl.ds(pl.multiple_of(r * SUBLANES, SUBLANES), SUBLANES), :]


def _router_kernel(h_ref, g_ref, wr_ref, hn_ref, meta_ref, total_ref, carry_ref):
    i = pl.program_id(0)
    tc = h_ref.shape[0]

    @pl.when(i == 0)
    def _():
        carry_ref[...] = jnp.zeros_like(carry_ref)

    @pl.when(i == pl.num_programs(0) - 1)
    def _():
        hn_ref[...] = jnp.zeros_like(hn_ref)

    @pl.when(i < pl.num_programs(0) - 1)
    def _():
        _route_chunk(h_ref, g_ref, wr_ref, hn_ref, meta_ref, total_ref, carry_ref)


def _route_chunk(h_ref, g_ref, wr_ref, hn_ref, meta_ref, total_ref, carry_ref):
    tc = h_ref.shape[0]
    hn = _rms(h_ref[...], g_ref[...])
    _store_row_tiles(hn_ref, hn)
    hn_hi = hn.astype(BF16)
    hn_lo = (hn - hn_hi.astype(F32)).astype(BF16)
    logits = (_dot(hn_hi, wr_ref[0]) + _dot(hn_lo, wr_ref[0]) + _dot(hn_hi, wr_ref[1]))
    lane = lax.broadcasted_iota(jnp.int32, (tc, LANES), 1)
    neg = jnp.float32(-jnp.inf)
    l1 = jnp.where(lane < N_EXPERTS, logits, neg)
    m1 = jnp.max(l1, axis=-1, keepdims=True)
    i1 = jnp.min(jnp.where(l1 == m1, lane, LANES), axis=-1, keepdims=True)
    l2 = jnp.where(lane == i1, neg, l1)
    m2 = jnp.max(l2, axis=-1, keepdims=True)
    i2 = jnp.min(jnp.where(l2 == m2, lane, LANES), axis=-1, keepdims=True)
    ex = jnp.exp(m2 - m1)
    w1 = 1.0 / (1.0 + ex)
    w2 = ex / (1.0 + ex)
    sel1 = lane == i1
    sel2 = lane == i2
    onehot = (sel1 | sel2).astype(BF16)
    r = lax.broadcasted_iota(jnp.int32, (tc, tc), 0)
    c = lax.broadcasted_iota(jnp.int32, (tc, tc), 1)
    lower = (c < r).astype(BF16)
    carry = carry_ref[...]
    prior = _dot(lower, onehot) + carry
    rank1 = jnp.sum(jnp.where(sel1, prior, 0.0), axis=-1, keepdims=True)
    rank2 = jnp.sum(jnp.where(sel2, prior, 0.0), axis=-1, keepdims=True)
    meta = jnp.where(lane == 0, i1.astype(F32), 0.0)
    meta = jnp.where(lane == 1, i2.astype(F32), meta)
    meta = jnp.where(lane == 2, rank1, meta)
    meta = jnp.where(lane == 3, rank2, meta)
    meta = jnp.where(lane == 4, w1, meta)
    meta = jnp.where(lane == 5, w2, meta)
    meta_ref[...] = meta
    new_carry = carry + jnp.sum(onehot.astype(F32), axis=0, keepdims=True)
    carry_ref[...] = new_carry
    total_ref[...] = new_carry


def _router(h2, g, router_w):
    t, d = h2.shape
    tc = TOK_CHUNK
    nc = t // tc
    wr = jnp.zeros((d, LANES), F32).at[:, :router_w.shape[1]].set(router_w)
    wr_hi = wr.astype(BF16)
    wr = jnp.stack([wr_hi, (wr - wr_hi.astype(F32)).astype(BF16)])
    last = nc - 1
    return pl.pallas_call(
        _router_kernel,
        grid=(nc + 1,),
        in_specs=[
            pl.BlockSpec((tc, d), lambda i: (jnp.minimum(i, last), 0)),
            _const_spec((1, d)),
            _const_spec((2, d, LANES)),
        ],
        out_specs=[
            pl.BlockSpec((tc * SUBLANES, LANES), lambda i: (i, 0)),
            pl.BlockSpec((tc, LANES), lambda i: (jnp.minimum(i, last), 0)),
            pl.BlockSpec((1, LANES), lambda i: (0, 0)),
        ],
        out_shape=[
            jax.ShapeDtypeStruct(((t + tc) * SUBLANES, LANES), F32),
            jax.ShapeDtypeStruct((t, LANES), F32),
            jax.ShapeDtypeStruct((1, LANES), F32),
        ],
        scratch_shapes=[pltpu.VMEM((1, LANES), F32)],
        compiler_params=_cparams("arbitrary"),
        name="moe_router",
    )(h2, g.reshape(1, d), wr)


def _row_copy(src, dst, sem):
    return pltpu.make_async_copy(src, dst, sem)


def _experts_kernel(te_ref, nt_ref, nreal_ref, inv_cur_ref, inv_nxt_ref, inv_prv_ref, hn_hbm,
                    wg_ref, wu_ref, wd_ref, y_hbm, xin_ref, x_ref, acc_ref, yout_ref,
                    gsem, ssem, *, n_tokens):
    q = pl.program_id(0)
    f = pl.program_id(1)
    nf = pl.num_programs(1)
    nt = nt_ref[0]
    rows = x_ref.shape[0]
    live = q < nt
    slot = q % 2
    other = 1 - slot
    share = -(-rows // MOE_STEPS)
    n_prev = jnp.where(q >= 1, nreal_ref[jnp.maximum(q - 1, 0)], 0)
    n_prev2 = nreal_ref[jnp.maximum(q - 2, 0)]

    def token_row(a):
        return jnp.where(a < 0, n_tokens, jnp.where(a >= n_tokens, a - n_tokens, a))

    def gather_row(src, dst_slot, j):
        _row_copy(_row_tile(hn_hbm, src), _row_tile(xin_ref.at[dst_slot], j), gsem.at[dst_slot]).start()

    def scatter_row(src_slot, j, dst):
        _row_copy(_row_tile(yout_ref.at[src_slot], j), _row_tile(y_hbm, dst), ssem.at[src_slot]).start()

    def wait_gathered(s):
        _row_copy(hn_hbm.at[pl.ds(0, rows * SUBLANES), :], xin_ref.at[s], gsem.at[s]).wait()

    def wait_scattered(s, n):
        size = pl.multiple_of(n * SUBLANES, SUBLANES)
        _row_copy(yout_ref.at[s, pl.ds(0, size), :], y_hbm.at[pl.ds(0, size), :], ssem.at[s]).wait()

    @pl.when((q == 0) & (f == 0))
    def _():
        def one(j, c):
            gather_row(token_row(inv_cur_ref[0, 0, j]), 0, j)
            return c
        lax.fori_loop(0, rows, one, 0, unroll=ROW_BATCH)

    @pl.when((q >= 2) & (q <= nt) & (f == 0))
    def _():
        wait_scattered(slot, n_prev2)

    @pl.when((q <= nt) & (f == 0))
    def _():
        wait_gathered(slot)

    @pl.when(live & (f == 0))
    def _():
        x_ref[...] = _load_row_tiles(xin_ref.at[slot]).astype(BF16)
        acc_ref[...] = jnp.zeros_like(acc_ref)

    @pl.when(live)
    def _():
        lo = f * share

        def issue_batch(b):
            base = lo + b * ROW_BATCH
            srcs = [token_row(inv_nxt_ref[0, 0, base + u]) for u in range(ROW_BATCH)]
            dsts = [inv_prv_ref[0, 0, base + u] for u in range(ROW_BATCH)]
            for u in range(ROW_BATCH):
                gather_row(srcs[u], other, base + u)
            for u in range(ROW_BATCH):
                @pl.when(base + u < n_prev)
                def _(u=u):
                    scatter_row(other, base + u, dsts[u])

        n_sub = wd_ref.shape[1] // FFN_CHUNK
        n_batches = share // ROW_BATCH
        x = x_ref[...]
        acc = acc_ref[...]
        for c in range(n_sub):
            for b in range(c * n_batches // n_sub, (c + 1) * n_batches // n_sub):
                issue_batch(b)
            cols = slice(c * FFN_CHUNK, (c + 1) * FFN_CHUNK)
            gate = _dot(x, wg_ref[0, :, cols])
            up = _dot(x, wu_ref[0, :, cols])
            act = (gate * jax.nn.sigmoid(gate) * up).astype(BF16)
            acc = acc + _dot(act, wd_ref[0, cols, :])
        acc_ref[...] = acc

    @pl.when(live & (f == nf - 1))
    def _():
        _store_row_tiles(yout_ref.at[slot], acc_ref[...])

    @pl.when((q == nt) & (f == 0))
    def _():
        def one(j, c):
            scatter_row(other, j, inv_prv_ref[0, 0, j])
            return c
        lax.fori_loop(0, n_prev, one, 0)
        wait_scattered(other, n_prev)


def _experts(hn_rows, inv, tile_expert, n_tiles, tile_real, w_gate, w_up, w_down, n_tokens):
    d = SUBLANES * LANES
    ne, ff, _ = w_down.shape
    assert w_down.shape[2] == d
    nf = ff // MOE_CHUNK
    assert nf == MOE_STEPS and MOE_ROWS % (MOE_STEPS * ROW_BATCH) == 0
    nq = inv.shape[0]

    def tile(q, nt):
        return jnp.minimum(q, nt[0] - 1)

    def chunk(q, f, nt):
        return jnp.where(q < nt[0], f, nf - 1)

    idx_block = (1, 1, MOE_ROWS)
    return pl.pallas_call(
        functools.partial(_experts_kernel, n_tokens=n_tokens),
        grid_spec=pltpu.PrefetchScalarGridSpec(
            num_scalar_prefetch=3,
            grid=(nq + 1, nf),
            in_specs=[
                pl.BlockSpec(idx_block, lambda q, f, te, nt, nr: (jnp.minimum(q, nq - 1), 0, 0),
                             memory_space=pltpu.SMEM),
                pl.BlockSpec(idx_block, lambda q, f, te, nt, nr: (jnp.minimum(q + 1, nq - 1), 0, 0),
                             memory_space=pltpu.SMEM),
                pl.BlockSpec(idx_block, lambda q, f, te, nt, nr: (jnp.clip(q - 1, 0, nq - 1), 0, 0),
                             memory_space=pltpu.SMEM),
                pl.BlockSpec(memory_space=pl.ANY),
                pl.BlockSpec((1, d, MOE_CHUNK),
                             lambda q, f, te, nt, nr: (te[tile(q, nt)], 0, chunk(q, f, nt))),
                pl.BlockSpec((1, d, MOE_CHUNK),
                             lambda q, f, te, nt, nr: (te[tile(q, nt)], 0, chunk(q, f, nt))),
                pl.BlockSpec((1, MOE_CHUNK, d),
                             lambda q, f, te, nt, nr: (te[tile(q, nt)], chunk(q, f, nt), 0)),
            ],
            out_specs=pl.BlockSpec(memory_space=pl.ANY),
            scratch_shapes=[
                pltpu.VMEM((2, MOE_ROWS * SUBLANES, LANES), F32),
                pltpu.VMEM((MOE_ROWS, d), BF16),
                pltpu.VMEM((MOE_ROWS, d), F32),
                pltpu.VMEM((2, MOE_ROWS * SUBLANES, LANES), F32),
                pltpu.SemaphoreType.DMA((2,)),
                pltpu.SemaphoreType.DMA((2,)),
            ],
        ),
        out_shape=jax.ShapeDtypeStruct((2 * n_tokens * SUBLANES, LANES), F32),
        compiler_params=pltpu.CompilerParams(
            dimension_semantics=("arbitrary", "arbitrary"), vmem_limit_bytes=VMEM_LIMIT,
            disable_bounds_checks=True),
        name="moe_experts",
    )(tile_expert, n_tiles, tile_real, inv, inv, inv, hn_rows, w_gate, w_up, w_down)


def _combine_kernel(h_ref, y1_ref, y2_ref, wts_ref, g_ref, o_ref):
    wt = wts_ref[...]
    moe = wt[:, 0:1] * _load_row_tiles(y1_ref) + wt[:, 1:2] * _load_row_tiles(y2_ref)
    o_ref[...] = _rms(h_ref[...] + moe, g_ref[...])


def _combine(h2, y, wts, g_final):
    t, d = h2.shape
    ts = SEQ_TILE
    nt = t // ts
    return pl.pallas_call(
        _combine_kernel,
        grid=(nt,),
        in_specs=[
            pl.BlockSpec((ts, d), lambda i: (i, 0)),
            pl.BlockSpec((ts * SUBLANES, LANES), lambda i: (i, 0)),
            pl.BlockSpec((ts * SUBLANES, LANES), lambda i: (nt + i, 0)),
            pl.BlockSpec((ts, 2), lambda i: (i, 0)),
            _const_spec((1, d)),
        ],
        out_specs=pl.BlockSpec((ts, d), lambda i: (i, 0)),
        out_shape=jax.ShapeDtypeStruct((t, d), F32),
        compiler_params=_cparams("parallel"),
        name="moe_combine",
    )(h2, y, y, wts, g_final.reshape(1, d))


def _moe_block(h2, g, router_w, w_gate, w_up, w_down, g_final):
    t, d = h2.shape
    ne = w_down.shape[0]
    nq = (2 * t) // MOE_ROWS + ne
    n_rows = nq * MOE_ROWS

    hn_rows, meta, total = _router(h2, g, router_w)

    experts = meta[:, 0:2].astype(jnp.int32)
    ranks = meta[:, 2:4].astype(jnp.int32)
    wts = meta[:, 4:6]
    counts = total[0, :ne].astype(jnp.int32)
    tiles_per = (counts + MOE_ROWS - 1) // MOE_ROWS
    tile_end = jnp.cumsum(tiles_per)
    tile_start = tile_end - tiles_per
    n_tiles = tile_end[-1]
    dest = (jnp.take(tile_start * MOE_ROWS, experts) + ranks).T.reshape(-1)
    qs = jnp.arange(nq, dtype=jnp.int32)
    tile_expert = jnp.minimum(jnp.sum((qs[:, None] >= tile_end[None, :]).astype(jnp.int32), axis=1),
                              ne - 1)
    tile_real = jnp.clip(jnp.take(counts, tile_expert) - (qs - jnp.take(tile_start, tile_expert)) * MOE_ROWS,
                         0, MOE_ROWS)
    inv = jnp.full((n_rows,), -1, jnp.int32).at[dest].set(jnp.arange(2 * t, dtype=jnp.int32))
    y = _experts(hn_rows, inv.reshape(nq, 1, MOE_ROWS), tile_expert, n_tiles.reshape(1), tile_real,
                 w_gate, w_up, w_down, t)
    return _combine(h2, y, wts, g_final)


def kernel(x, mem, ev_norm_mix, ev_w_in, ev_conv_a, ev_pool_w, ev_pool_scale, ev_w_out, ev_norm_ffn, ev_ffn_gu, ev_ffn_down, od_norm_mix, od_pw1_w, od_pw1_b, od_dw_w, od_dw_b, od_ln_g, od_ln_b, od_pw2_w, od_pw2_b, od_norm_moe, od_router, od_moe_gu, od_moe_down, xa_norm, xa_norm_mem, xa_wq, xa_wkv, xa_wo, final_norm):
    b, s, d = x.shape
    assert ev_w_in.shape[0] == 1 and od_pw1_w.shape[0] == 1 and xa_wq.shape[0] == 2
    kv = _memory_kv(mem, xa_norm_mem, xa_wkv)
    n_steps = (b * s) // SEQ_TILE
    h, moe_down = _even_mixer(x, ev_norm_mix[0], ev_w_in[0], ev_conv_a[0], ev_pool_w[0],
                              ev_pool_scale[0], ev_w_out[0], _CastRider(od_moe_down[0], n_steps))
    h = _cross_attn(h, xa_norm[0], xa_wq[0], kv[0], xa_wo[0])
    h, moe_gate = _dense_swiglu(h.reshape(b * s, d), ev_norm_ffn[0], ev_ffn_gu[0], ev_ffn_down[0],
                                _CastRider(od_moe_gu[0], n_steps, col_blocks=2, col_block=0))
    h, moe_up = _conformer(h.reshape(b, s, d), od_norm_mix[0], od_pw1_w[0], od_pw1_b[0], od_dw_w[0],
                           od_dw_b[0], od_ln_g[0], od_ln_b[0], od_pw2_w[0], od_pw2_b[0],
                           _CastRider(od_moe_gu[0], n_steps, col_blocks=2, col_block=1))
    h = _cross_attn(h, xa_norm[1], xa_wq[1], kv[1], xa_wo[1])
    out = _moe_block(h.reshape(b * s, d), od_norm_moe[0], od_router[0], moe_gate, moe_up,
                     moe_down, final_norm)
    return out.reshape(b, s, d)
```

```python
import functools

import jax
import jax.numpy as jnp
from jax import lax
from jax.experimental import pallas as pl
from jax.experimental.pallas import tpu as pltpu

EPS = 1e-6
BF16 = jnp.bfloat16
F32 = jnp.float32

POOL_WINDOWS = (2, 4, 8, 16)
XATTN_HEADS = 4
N_EXPERTS = 8
LANES = 128
SUBLANES = 8
VMEM_LIMIT = 56 * 1024 * 1024

SEQ_TILE = 512
MIX_HALO = 8
CONF_HALO = 16
CONV_ROWS = 32
GLU_PITCH = 2
FFN_CHUNK = 256
MOE_ROWS = 512
MOE_CHUNK = 1792
MOE_STEPS = 2
ROW_BATCH = 8
TOK_CHUNK = 512


def _rms(x, g):
    return x * lax.rsqrt(jnp.mean(x * x, axis=-1, keepdims=True) + EPS) * g


def _dot(a, b):
    return jnp.dot(a, b, preferred_element_type=F32)


def _cparams(*sem):
    return pltpu.CompilerParams(dimension_semantics=sem, vmem_limit_bytes=VMEM_LIMIT)


def _const_spec(shape):
    nd = len(shape)
    return pl.BlockSpec(shape, lambda *_: (0,) * nd, pipeline_mode=pl.Buffered(1))


class _CastRider:
    def __init__(self, w, n_steps, col_blocks=1, col_block=0):
        e, r, c = w.shape
        per_expert = n_steps // e
        self.w = w
        self.block = (1, r // per_expert, c // col_blocks)
        self.out_shape = jax.ShapeDtypeStruct((e, r, c // col_blocks), BF16)
        self._per_expert = per_expert
        self._col_block = col_block

    def spec(self, step_of):
        pe, cb = self._per_expert, self._col_block

        def in_map(*idx):
            s = step_of(*idx)
            return (s // pe, s % pe, cb)

        def out_map(*idx):
            s = step_of(*idx)
            return (s // pe, s % pe, 0)

        return pl.BlockSpec(self.block, in_map), pl.BlockSpec(self.block, out_map)


def _cast_block(src_ref, dst_ref):
    dst_ref[...] = src_ref[...].astype(BF16)


def _even_mixer_kernel(x_ref, xp_ref, xn_ref, g_ref, win_ref, conv_ref, pw_ref,
                       ps_ref, wout_ref, cast_src_ref, o_ref, cast_dst_ref, z_ref, a_ref, *,
                       seq_len):
    _cast_block(cast_src_ref, cast_dst_ref)
    i = pl.program_id(1)
    n_i = pl.num_programs(1)
    ts = x_ref.shape[1]
    h0 = MIX_HALO
    aw = conv_ref.shape[1]
    bg = pw_ref.shape[1]
    xm = x_ref[0]
    xa = jnp.concatenate([xp_ref[0], xm, xn_ref[0]], axis=0)
    hn = _rms(xa, g_ref[...]).astype(BF16)
    z = _dot(hn, win_ref[...])
    row = lax.broadcasted_iota(jnp.int32, (ts + 2 * h0, 1), 0)
    inside = ((row >= h0) | (i > 0)) & ((row < ts + h0) | (i < n_i - 1))
    z_ref[...] = jnp.where(inside, z, 0.0)

    a_ref[...] = z_ref[:, 2 * aw:3 * aw] * z_ref[:, 0:aw]
    cw = conv_ref[...]
    conv = (cw[0:1] * a_ref[h0 - 1:h0 - 1 + ts, :] + cw[1:2] * a_ref[h0:h0 + ts, :]
            + cw[2:3] * a_ref[h0 + 1:h0 + 1 + ts, :])
    parts = [(z_ref[h0:h0 + ts, aw:2 * aw] * conv).astype(BF16)]

    pos = i * ts + lax.broadcasted_iota(jnp.int32, (ts, 1), 0)
    for gi, win in enumerate(POOL_WINDOWS):
        left = win // 2
        right = win - 1 - left
        c0 = 3 * aw + gi * bg
        s = z_ref[h0 - left:h0 - left + ts, c0:c0 + bg]
        for j in range(-left + 1, right + 1):
            s = s + z_ref[h0 + j:h0 + j + ts, c0:c0 + bg]
        cnt = (jnp.minimum(pos + right, seq_len - 1) - jnp.maximum(pos - left, 0) + 1).astype(F32)
        p = s / cnt - z_ref[h0:h0 + ts, c0:c0 + bg]
        yb = _dot(p.astype(BF16), pw_ref[gi]) * ps_ref[:, gi * bg:(gi + 1) * bg]
        parts.append(yb.astype(BF16))
    y = jnp.concatenate(parts, axis=1)
    o_ref[0] = xm + _dot(y, wout_ref[...])


def _even_mixer(x, g, w_in, conv_a, pool_w, pool_scale, w_out, rider):
    b, s, d = x.shape
    ts, h0 = SEQ_TILE, MIX_HALO
    rider_in, rider_out = rider.spec(lambda bi, i: bi * (s // ts) + i)
    r = ts // h0
    incols = w_in.shape[1]
    aw = conv_a.shape[1]
    return pl.pallas_call(
        functools.partial(_even_mixer_kernel, seq_len=s),
        grid=(b, s // ts),
        in_specs=[
            pl.BlockSpec((1, ts, d), lambda bi, i: (bi, i, 0)),
            pl.BlockSpec((1, h0, d), lambda bi, i: (bi, jnp.maximum(i * r - 1, 0), 0)),
            pl.BlockSpec((1, h0, d), lambda bi, i: (bi, jnp.minimum((i + 1) * r, s // h0 - 1), 0)),
            _const_spec((1, d)),
            _const_spec((d, incols)),
            _const_spec(conv_a.shape),
            _const_spec(pool_w.shape),
            _const_spec((1, pool_scale.shape[-1])),
            _const_spec(w_out.shape),
            rider_in,
        ],
        out_specs=[pl.BlockSpec((1, ts, d), lambda bi, i: (bi, i, 0)), rider_out],
        out_shape=[jax.ShapeDtypeStruct((b, s, d), F32), rider.out_shape],
        scratch_shapes=[pltpu.VMEM((ts + 2 * h0, incols), F32),
                        pltpu.VMEM((ts + 2 * h0, aw), F32)],
        compiler_params=_cparams("parallel", "parallel"),
        name="even_mixer",
    )(x, x, x, g.reshape(1, d), w_in.astype(BF16), conv_a, pool_w.astype(BF16),
      pool_scale.reshape(1, -1), w_out.astype(BF16), rider.w)


def _conformer_kernel(x_ref, xp_ref, xn_ref, g_ref, pw1_ref, b1_ref, dw_ref, dwb_ref,
                      lng_ref, lnb_ref, pw2_ref, b2_ref, cast_src_ref, o_ref, cast_dst_ref,
                      glu_ref, conv_ref):
    _cast_block(cast_src_ref, cast_dst_ref)
    i = pl.program_id(1)
    n_i = pl.num_programs(1)
    ts = x_ref.shape[1]
    d = x_ref.shape[2]
    h0 = CONF_HALO
    taps = dw_ref.shape[0] // SUBLANES
    half = taps // 2
    xm = x_ref[0]
    xa = jnp.concatenate([xp_ref[0], xm, xn_ref[0]], axis=0)
    hn = _rms(xa, g_ref[...]).astype(BF16)
    y1 = _dot(hn, pw1_ref[...]) + b1_ref[...]
    glu = y1[:, :d] * jax.nn.sigmoid(y1[:, d:])
    row = lax.broadcasted_iota(jnp.int32, (ts + 2 * h0, 1), 0)
    inside = ((row >= h0) | (i > 0)) & ((row < ts + h0) | (i < n_i - 1))
    glu = jnp.where(inside, glu, 0.0)
    n_rows = ts + 2 * h0
    for c in range(d // LANES):
        glu_ref[c, pl.ds(0, n_rows, stride=GLU_PITCH), :] = glu[:, c * LANES:(c + 1) * LANES]

    n_sub = CONV_ROWS // SUBLANES
    for c in range(d // LANES):
        cols = slice(c * LANES, (c + 1) * LANES)
        w_taps = [dw_ref[SUBLANES * k:SUBLANES * (k + 1), cols] for k in range(taps)]
        bias = dwb_ref[:, cols]

        def conv_block(rb, carry, c=c, cols=cols, w_taps=w_taps, bias=bias):
            base = pl.multiple_of(rb * CONV_ROWS, CONV_ROWS)
            accs = [jnp.zeros((SUBLANES, LANES), F32) + bias for _ in range(n_sub)]
            for k in range(taps):
                first = base + (h0 - half + k)
                for a in range(n_sub):
                    rows8 = pl.ds((first + a * SUBLANES) * GLU_PITCH, SUBLANES, stride=GLU_PITCH)
                    accs[a] = accs[a] + w_taps[k] * glu_ref[c, rows8, :]
            conv_ref[pl.ds(base, CONV_ROWS), cols] = jnp.concatenate(accs, axis=0)
            return carry

        lax.fori_loop(0, ts // CONV_ROWS, conv_block, 0)

    acc = conv_ref[...]
    mu = jnp.mean(acc, axis=-1, keepdims=True)
    xc = acc - mu
    var = jnp.mean(xc * xc, axis=-1, keepdims=True)
    yn = xc * lax.rsqrt(var + EPS) * lng_ref[...] + lnb_ref[...]
    act = (yn * jax.nn.sigmoid(yn)).astype(BF16)
    o_ref[0] = xm + _dot(act, pw2_ref[...]) + b2_ref[...]


def _conformer(x, g, pw1_w, pw1_b, dw_w, dw_b, ln_g, ln_b, pw2_w, pw2_b, rider):
    b, s, d = x.shape
    ts, h0 = SEQ_TILE, CONF_HALO
    rider_in, rider_out = rider.spec(lambda bi, i: bi * (s // ts) + i)
    r = ts // h0
    return pl.pallas_call(
        _conformer_kernel,
        grid=(b, s // ts),
        in_specs=[
            pl.BlockSpec((1, ts, d), lambda bi, i: (bi, i, 0)),
            pl.BlockSpec((1, h0, d), lambda bi, i: (bi, jnp.maximum(i * r - 1, 0), 0)),
            pl.BlockSpec((1, h0, d), lambda bi, i: (bi, jnp.minimum((i + 1) * r, s // h0 - 1), 0)),
            _const_spec((1, d)),
            _const_spec(pw1_w.shape),
            _const_spec((1, 2 * d)),
            _const_spec((dw_w.shape[0] * SUBLANES, d)),
            _const_spec((1, d)),
            _const_spec((1, d)),
            _const_spec((1, d)),
            _const_spec(pw2_w.shape),
            _const_spec((1, d)),
            rider_in,
        ],
        out_specs=[pl.BlockSpec((1, ts, d), lambda bi, i: (bi, i, 0)), rider_out],
        out_shape=[jax.ShapeDtypeStruct((b, s, d), F32), rider.out_shape],
        scratch_shapes=[pltpu.VMEM((d // LANES, GLU_PITCH * (ts + 2 * h0), LANES), F32),
                        pltpu.VMEM((ts, d), F32)],
        compiler_params=_cparams("parallel", "parallel"),
        name="conformer",
    )(x, x, x, g.reshape(1, d), pw1_w.astype(BF16), pw1_b.reshape(1, -1),
      jnp.repeat(dw_w, SUBLANES, axis=0),
      dw_b.reshape(1, d), ln_g.reshape(1, d), ln_b.reshape(1, d), pw2_w.astype(BF16),
      pw2_b.reshape(1, d), rider.w)


def _kv_kernel(mem_ref, g_ref, wkv_ref, kv_ref):
    mn = _rms(mem_ref[0], g_ref[0]).astype(BF16)
    kv_ref[0, 0] = _dot(mn, wkv_ref[0]).astype(BF16)


def _memory_kv(mem, norm_mem, wkv):
    b, m, d = mem.shape
    nl = wkv.shape[0]
    return pl.pallas_call(
        _kv_kernel,
        grid=(nl, b),
        in_specs=[
            pl.BlockSpec((1, m, d), lambda l, bi: (bi, 0, 0)),
            pl.BlockSpec((1, 1, d), lambda l, bi: (l, 0, 0)),
            pl.BlockSpec((1, d, 2 * d), lambda l, bi: (l, 0, 0)),
        ],
        out_specs=pl.BlockSpec((1, 1, m, 2 * d), lambda l, bi: (l, bi, 0, 0)),
        out_shape=jax.ShapeDtypeStruct((nl, b, m, 2 * d), BF16),
        compiler_params=_cparams("parallel", "parallel"),
        name="memory_kv",
    )(mem, norm_mem.reshape(nl, 1, d), wkv.astype(BF16))


def _xattn_kernel(h_ref, g_ref, wq_ref, kv_ref, wo_ref, o_ref):
    d = h_ref.shape[2]
    hd = d // XATTN_HEADS
    hb = h_ref[0]
    hn = _rms(hb, g_ref[...]).astype(BF16)
    q = (_dot(hn, wq_ref[...]) * (hd ** -0.5)).astype(BF16)
    kv = kv_ref[0]
    outs = []
    for a in range(XATTN_HEADS):
        qh = q[:, a * hd:(a + 1) * hd]
        kh = kv[:, a * hd:(a + 1) * hd]
        vh = kv[:, d + a * hd:d + (a + 1) * hd]
        sc = lax.dot_general(qh, kh, (((1,), (1,)), ((), ())), preferred_element_type=F32)
        e = jnp.exp(sc - jnp.max(sc, axis=-1, keepdims=True))
        o = _dot(e.astype(BF16), vh) / jnp.sum(e, axis=-1, keepdims=True)
        outs.append(o.astype(BF16))
    o_ref[0] = hb + _dot(jnp.concatenate(outs, axis=1), wo_ref[...])


def _cross_attn(h, g, wq, kv, wo):
    b, s, d = h.shape
    ts = SEQ_TILE
    m = kv.shape[1]
    return pl.pallas_call(
        _xattn_kernel,
        grid=(b, s // ts),
        in_specs=[
            pl.BlockSpec((1, ts, d), lambda bi, i: (bi, i, 0)),
            _const_spec((1, d)),
            _const_spec((d, d)),
            pl.BlockSpec((1, m, 2 * d), lambda bi, i: (bi, 0, 0)),
            _const_spec((d, d)),
        ],
        out_specs=pl.BlockSpec((1, ts, d), lambda bi, i: (bi, i, 0)),
        out_shape=jax.ShapeDtypeStruct((b, s, d), F32),
        compiler_params=_cparams("parallel", "parallel"),
        name="cross_attn",
    )(h, g.reshape(1, d), wq.astype(BF16), kv, wo.astype(BF16))


def _swiglu_kernel(h_ref, g_ref, wgu_ref, wd_ref, cast_src_ref, o_ref, cast_dst_ref):
    _cast_block(cast_src_ref, cast_dst_ref)
    ff = wd_ref.shape[0]
    hb = h_ref[...]
    hn = _rms(hb, g_ref[...]).astype(BF16)
    acc = hb
    for c in range(ff // FFN_CHUNK):
        lo = c * FFN_CHUNK
        gate = _dot(hn, wgu_ref[:, lo:lo + FFN_CHUNK])
        up = _dot(hn, wgu_ref[:, ff + lo:ff + lo + FFN_CHUNK])
        act = (gate * jax.nn.sigmoid(gate) * up).astype(BF16)
        acc = acc + _dot(act, wd_ref[lo:lo + FFN_CHUNK, :])
    o_ref[...] = acc


def _dense_swiglu(h2, g, w_gu, w_down, rider):
    t, d = h2.shape
    ts = SEQ_TILE
    rider_in, rider_out = rider.spec(lambda i: i)
    return pl.pallas_call(
        _swiglu_kernel,
        grid=(t // ts,),
        in_specs=[
            pl.BlockSpec((ts, d), lambda i: (i, 0)),
            _const_spec((1, d)),
            _const_spec(w_gu.shape),
            _const_spec(w_down.shape),
            rider_in,
        ],
        out_specs=[pl.BlockSpec((ts, d), lambda i: (i, 0)), rider_out],
        out_shape=[jax.ShapeDtypeStruct((t, d), F32), rider.out_shape],
        compiler_params=_cparams("parallel"),
        name="dense_swiglu",
    )(h2, g.reshape(1, d), w_gu.astype(BF16), w_down.astype(BF16), rider.w)


def _store_row_tiles(ref, x):
    for s in range(SUBLANES):
        ref[pl.ds(s, x.shape[0], stride=SUBLANES), :] = x[:, s * LANES:(s + 1) * LANES]


def _load_row_tiles(ref):
    rows = ref.shape[0] // SUBLANES
    return jnp.concatenate([ref[pl.ds(s, rows, stride=SUBLANES), :] for s in range(SUBLANES)], axis=1)


def _row_tile(ref, r):
    return ref.at[pl.ds(pl.multiple_of(r * SUBLANES, SUBLANES), SUBLANES), :]


def _router_kernel(h_ref, g_ref, wr_ref, hn_ref, meta_ref, total_ref, carry_ref):
    i = pl.program_id(0)
    tc = h_ref.shape[0]

    @pl.when(i == 0)
    def _():
        carry_ref[...] = jnp.zeros_like(carry_ref)

    @pl.when(i == pl.num_programs(0) - 1)
    def _():
        hn_ref[...] = jnp.zeros_like(hn_ref)

    @pl.when(i < pl.num_programs(0) - 1)
    def _():
        _route_chunk(h_ref, g_ref, wr_ref, hn_ref, meta_ref, total_ref, carry_ref)


def _route_chunk(h_ref, g_ref, wr_ref, hn_ref, meta_ref, total_ref, carry_ref):
    tc = h_ref.shape[0]
    hn = _rms(h_ref[...], g_ref[...])
    _store_row_tiles(hn_ref, hn)
    hn_hi = hn.astype(BF16)
    hn_lo = (hn - hn_hi.astype(F32)).astype(BF16)
    logits = (_dot(hn_hi, wr_ref[0]) + _dot(hn_lo, wr_ref[0]) + _dot(hn_hi, wr_ref[1]))
    lane = lax.broadcasted_iota(jnp.int32, (tc, LANES), 1)
    neg = jnp.float32(-jnp.inf)
    l1 = jnp.where(lane < N_EXPERTS, logits, neg)
    m1 = jnp.max(l1, axis=-1, keepdims=True)
    i1 = jnp.min(jnp.where(l1 == m1, lane, LANES), axis=-1, keepdims=True)
    l2 = jnp.where(lane == i1, neg, l1)
    m2 = jnp.max(l2, axis=-1, keepdims=True)
    i2 = jnp.min(jnp.where(l2 == m2, lane, LANES), axis=-1, keepdims=True)
    ex = jnp.exp(m2 - m1)
    w1 = 1.0 / (1.0 + ex)
    w2 = ex / (1.0 + ex)
    sel1 = lane == i1
    sel2 = lane == i2
    onehot = (sel1 | sel2).astype(BF16)
    r = lax.broadcasted_iota(jnp.int32, (tc, tc), 0)
    c = lax.broadcasted_iota(jnp.int32, (tc, tc), 1)
    lower = (c < r).astype(BF16)
    carry = carry_ref[...]
    prior = _dot(lower, onehot) + carry
    rank1 = jnp.sum(jnp.where(sel1, prior, 0.0), axis=-1, keepdims=True)
    rank2 = jnp.sum(jnp.where(sel2, prior, 0.0), axis=-1, keepdims=True)
    meta = jnp.where(lane == 0, i1.astype(F32), 0.0)
    meta = jnp.where(lane == 1, i2.astype(F32), meta)
    meta = jnp.where(lane == 2, rank1, meta)
    meta = jnp.where(lane == 3, rank2, meta)
    meta = jnp.where(lane == 4, w1, meta)
    meta = jnp.where(lane == 5, w2, meta)
    meta_ref[...] = meta
    new_carry = carry + jnp.sum(onehot.astype(F32), axis=0, keepdims=True)
    carry_ref[...] = new_carry
    total_ref[...] = new_carry


def _router(h2, g, router_w):
    t, d = h2.shape
    tc = TOK_CHUNK
    nc = t // tc
    wr = jnp.zeros((d, LANES), F32).at[:, :router_w.shape[1]].set(router_w)
    wr_hi = wr.astype(BF16)
    wr = jnp.stack([wr_hi, (wr - wr_hi.astype(F32)).astype(BF16)])
    last = nc - 1
    return pl.pallas_call(
        _router_kernel,
        grid=(nc + 1,),
        in_specs=[
            pl.BlockSpec((tc, d), lambda i: (jnp.minimum(i, last), 0)),
            _const_spec((1, d)),
            _const_spec((2, d, LANES)),
        ],
        out_specs=[
            pl.BlockSpec((tc * SUBLANES, LANES), lambda i: (i, 0)),
            pl.BlockSpec((tc, LANES), lambda i: (jnp.minimum(i, last), 0)),
            pl.BlockSpec((1, LANES), lambda i: (0, 0)),
        ],
        out_shape=[
            jax.ShapeDtypeStruct(((t + tc) * SUBLANES, LANES), F32),
            jax.ShapeDtypeStruct((t, LANES), F32),
            jax.ShapeDtypeStruct((1, LANES), F32),
        ],
        scratch_shapes=[pltpu.VMEM((1, LANES), F32)],
        compiler_params=_cparams("arbitrary"),
        name="moe_router",
    )(h2, g.reshape(1, d), wr)


def _row_copy(src, dst, sem):
    return pltpu.make_async_copy(src, dst, sem)


def _experts_kernel(te_ref, nt_ref, nreal_ref, inv_cur_ref, inv_nxt_ref, inv_prv_ref, hn_hbm,
                    wg_ref, wu_ref, wd_ref, y_hbm, xin_ref, x_ref, acc_ref, yout_ref,
                    gsem, ssem, *, n_tokens):
    q = pl.program_id(0)
    f = pl.program_id(1)
    nf = pl.num_programs(1)
    nt = nt_ref[0]
    rows = x_ref.shape[0]
    live = q < nt
    slot = q % 2
    other = 1 - slot
    share = -(-rows // MOE_STEPS)
    n_prev = jnp.where(q >= 1, nreal_ref[jnp.maximum(q - 1, 0)], 0)
    n_prev2 = nreal_ref[jnp.maximum(q - 2, 0)]

    def token_row(a):
        return jnp.where(a < 0, n_tokens, jnp.where(a >= n_tokens, a - n_tokens, a))

    def gather_row(src, dst_slot, j):
        _row_copy(_row_tile(hn_hbm, src), _row_tile(xin_ref.at[dst_slot], j), gsem.at[dst_slot]).start()

    def scatter_row(src_slot, j, dst):
        _row_copy(_row_tile(yout_ref.at[src_slot], j), _row_tile(y_hbm, dst), ssem.at[src_slot]).start()

    def wait_gathered(s):
        _row_copy(hn_hbm.at[pl.ds(0, rows * SUBLANES), :], xin_ref.at[s], gsem.at[s]).wait()

    def wait_scattered(s, n):
        size = pl.multiple_of(n * SUBLANES, SUBLANES)
        _row_copy(yout_ref.at[s, pl.ds(0, size), :], y_hbm.at[pl.ds(0, size), :], ssem.at[s]).wait()

    @pl.when((q == 0) & (f == 0))
    def _():
        def one(j, c):
            gather_row(token_row(inv_cur_ref[0, 0, j]), 0, j)
            return c
        lax.fori_loop(0, rows, one, 0, unroll=ROW_BATCH)

    @pl.when((q >= 2) & (q <= nt) & (f == 0))
    def _():
        wait_scattered(slot, n_prev2)

    @pl.when((q <= nt) & (f == 0))
    def _():
        wait_gathered(slot)

    @pl.when(live & (f == 0))
    def _():
        x_ref[...] = _load_row_tiles(xin_ref.at[slot]).astype(BF16)
        acc_ref[...] = jnp.zeros_like(acc_ref)

    @pl.when(live)
    def _():
        lo = f * share

        def issue_batch(b):
            base = lo + b * ROW_BATCH
            srcs = [token_row(inv_nxt_ref[0, 0, base + u]) for u in range(ROW_BATCH)]
            dsts = [inv_prv_ref[0, 0, base + u] for u in range(ROW_BATCH)]
            for u in range(ROW_BATCH):
                gather_row(srcs[u], other, base + u)
            for u in range(ROW_BATCH):
                @pl.when(base + u < n_prev)
                def _(u=u):
                    scatter_row(other, base + u, dsts[u])

        n_sub = wd_ref.shape[1] // FFN_CHUNK
        n_batches = share // ROW_BATCH
        n_issue = max(1, n_sub - 3)
        x = x_ref[...]
        acc = acc_ref[...]
        for c in range(n_sub):
            if c < n_issue:
                for b in range(c * n_batches // n_issue, (c + 1) * n_batches // n_issue):
                    issue_batch(b)
            cols = slice(c * FFN_CHUNK, (c + 1) * FFN_CHUNK)
            gate = _dot(x, wg_ref[0, :, cols])
            up = _dot(x, wu_ref[0, :, cols])
            act = (gate * jax.nn.sigmoid(gate) * up).astype(BF16)
            acc = acc + _dot(act, wd_ref[0, cols, :])
        acc_ref[...] = acc

    @pl.when(live & (f == nf - 1))
    def _():
        _store_row_tiles(yout_ref.at[slot], acc_ref[...])

    @pl.when((q == nt) & (f == 0))
    def _():
        def one(j, c):
            scatter_row(other, j, inv_prv_ref[0, 0, j])
            return c
        lax.fori_loop(0, n_prev, one, 0)
        wait_scattered(other, n_prev)


def _experts(hn_rows, inv, tile_expert, n_tiles, tile_real, w_gate, w_up, w_down, n_tokens):
    d = SUBLANES * LANES
    ne, ff, _ = w_down.shape
    assert w_down.shape[2] == d
    nf = ff // MOE_CHUNK
    assert nf == MOE_STEPS and MOE_ROWS % (MOE_STEPS * ROW_BATCH) == 0
    nq = inv.shape[0]

    def tile(q, nt):
        return jnp.minimum(q, nt[0] - 1)

    def chunk(q, f, nt):
        return jnp.where(q < nt[0], f, nf - 1)

    idx_block = (1, 1, MOE_ROWS)
    return pl.pallas_call(
        functools.partial(_experts_kernel, n_tokens=n_tokens),
        grid_spec=pltpu.PrefetchScalarGridSpec(
            num_scalar_prefetch=3,
            grid=(nq + 1, nf),
            in_specs=[
                pl.BlockSpec(idx_block, lambda q, f, te, nt, nr: (jnp.minimum(q, nq - 1), 0, 0),
                             memory_space=pltpu.SMEM),
                pl.BlockSpec(idx_block, lambda q, f, te, nt, nr: (jnp.minimum(q + 1, nq - 1), 0, 0),
                             memory_space=pltpu.SMEM),
                pl.BlockSpec(idx_block, lambda q, f, te, nt, nr: (jnp.clip(q - 1, 0, nq - 1), 0, 0),
                             memory_space=pltpu.SMEM),
                pl.BlockSpec(memory_space=pl.ANY),
                pl.BlockSpec((1, d, MOE_CHUNK),
                             lambda q, f, te, nt, nr: (te[tile(q, nt)], 0, chunk(q, f, nt))),
                pl.BlockSpec((1, d, MOE_CHUNK),
                             lambda q, f, te, nt, nr: (te[tile(q, nt)], 0, chunk(q, f, nt))),
                pl.BlockSpec((1, MOE_CHUNK, d),
                             lambda q, f, te, nt, nr: (te[tile(q, nt)], chunk(q, f, nt), 0)),
            ],
            out_specs=pl.BlockSpec(memory_space=pl.ANY),
            scratch_shapes=[
                pltpu.VMEM((2, MOE_ROWS * SUBLANES, LANES), F32),
                pltpu.VMEM((MOE_ROWS, d), BF16),
                pltpu.VMEM((MOE_ROWS, d), F32),
                pltpu.VMEM((2, MOE_ROWS * SUBLANES, LANES), F32),
                pltpu.SemaphoreType.DMA((2,)),
                pltpu.SemaphoreType.DMA((2,)),
            ],
        ),
        out_shape=jax.ShapeDtypeStruct((2 * n_tokens * SUBLANES, LANES), F32),
        compiler_params=pltpu.CompilerParams(
            dimension_semantics=("arbitrary", "arbitrary"), vmem_limit_bytes=VMEM_LIMIT,
            disable_bounds_checks=True),
        name="moe_experts",
    )(tile_expert, n_tiles, tile_real, inv, inv, inv, hn_rows, w_gate, w_up, w_down)


def _invert_kernel(dest_ref, inv_ref):
    def fill(b, c):
        for u in range(ROW_BATCH):
            inv_ref[b * ROW_BATCH + u] = jnp.int32(-1)
        return c

    lax.fori_loop(0, inv_ref.shape[0] // ROW_BATCH, fill, 0)

    def put(b, c):
        a0 = b * ROW_BATCH
        rows = [dest_ref[a0 + u] for u in range(ROW_BATCH)]
        for u in range(ROW_BATCH):
            inv_ref[rows[u]] = a0 + u
        return c

    lax.fori_loop(0, dest_ref.shape[0] // ROW_BATCH, put, 0)


def _invert(dest, n_rows):
    assert dest.shape[0] % ROW_BATCH == 0 and n_rows % ROW_BATCH == 0
    return pl.pallas_call(
        _invert_kernel,
        in_specs=[pl.BlockSpec(memory_space=pltpu.SMEM)],
        out_specs=pl.BlockSpec(memory_space=pltpu.SMEM),
        out_shape=jax.ShapeDtypeStruct((n_rows,), jnp.int32),
        name="moe_invert",
    )(dest)


def _combine_kernel(h_ref, y1_ref, y2_ref, wts_ref, g_ref, o_ref):
    wt = wts_ref[...]
    moe = wt[:, 0:1] * _load_row_tiles(y1_ref) + wt[:, 1:2] * _load_row_tiles(y2_ref)
    o_ref[...] = _rms(h_ref[...] + moe, g_ref[...])


def _combine(h2, y, wts, g_final):
    t, d = h2.shape
    ts = SEQ_TILE
    nt = t // ts
    return pl.pallas_call(
        _combine_kernel,
        grid=(nt,),
        in_specs=[
            pl.BlockSpec((ts, d), lambda i: (i, 0)),
            pl.BlockSpec((ts * SUBLANES, LANES), lambda i: (i, 0)),
            pl.BlockSpec((ts * SUBLANES, LANES), lambda i: (nt + i, 0)),
            pl.BlockSpec((ts, 2), lambda i: (i, 0)),
            _const_spec((1, d)),
        ],
        out_specs=pl.BlockSpec((ts, d), lambda i: (i, 0)),
        out_shape=jax.ShapeDtypeStruct((t, d), F32),
        compiler_params=_cparams("parallel"),
        name="moe_combine",
    )(h2, y, y, wts, g_final.reshape(1, d))


def _moe_block(h2, g, router_w, w_gate, w_up, w_down, g_final):
    t, d = h2.shape
    ne = w_down.shape[0]
    nq = (2 * t) // MOE_ROWS + ne
    n_rows = nq * MOE_ROWS

    hn_rows, meta, total = _router(h2, g, router_w)

    experts = meta[:, 0:2].astype(jnp.int32)
    ranks = meta[:, 2:4].astype(jnp.int32)
    wts = meta[:, 4:6]
    counts = total[0, :ne].astype(jnp.int32)
    tiles_per = (counts + MOE_ROWS - 1) // MOE_ROWS
    tile_end = jnp.cumsum(tiles_per)
    tile_start = tile_end - tiles_per
    n_tiles = tile_end[-1]
    dest = (jnp.take(tile_start * MOE_ROWS, experts) + ranks).T.reshape(-1)
    qs = jnp.arange(nq, dtype=jnp.int32)
    tile_expert = jnp.minimum(jnp.sum((qs[:, None] >= tile_end[None, :]).astype(jnp.int32), axis=1),
                              ne - 1)
    tile_real = jnp.clip(jnp.take(counts, tile_expert) - (qs - jnp.take(tile_start, tile_expert)) * MOE_ROWS,
                         0, MOE_ROWS)
    inv = _invert(dest, n_rows)
    y = _experts(hn_rows, inv.reshape(nq, 1, MOE_ROWS), tile_expert, n_tiles.reshape(1), tile_real,
                 w_gate, w_up, w_down, t)
    return _combine(h2, y, wts, g_final)


def kernel(x, mem, ev_norm_mix, ev_w_in, ev_conv_a, ev_pool_w, ev_pool_scale, ev_w_out, ev_norm_ffn, ev_ffn_gu, ev_ffn_down, od_norm_mix, od_pw1_w, od_pw1_b, od_dw_w, od_dw_b, od_ln_g, od_ln_b, od_pw2_w, od_pw2_b, od_norm_moe, od_router, od_moe_gu, od_moe_down, xa_norm, xa_norm_mem, xa_wq, xa_wkv, xa_wo, final_norm):
    b, s, d = x.shape
    assert ev_w_in.shape[0] == 1 and od_pw1_w.shape[0] == 1 and xa_wq.shape[0] == 2
    kv = _memory_kv(mem, xa_norm_mem, xa_wkv)
    n_steps = (b * s) // SEQ_TILE
    h, moe_down = _even_mixer(x, ev_norm_mix[0], ev_w_in[0], ev_conv_a[0], ev_pool_w[0],
                              ev_pool_scale[0], ev_w_out[0], _CastRider(od_moe_down[0], n_steps))
    h = _cross_attn(h, xa_norm[0], xa_wq[0], kv[0], xa_wo[0])
    h, moe_gate = _dense_swiglu(h.reshape(b * s, d), ev_norm_ffn[0], ev_ffn_gu[0], ev_ffn_down[0],
                                _CastRider(od_moe_gu[0], n_steps, col_blocks=2, col_block=0))
    h, moe_up = _conformer(h.reshape(b, s, d), od_norm_mix[0], od_pw1_w[0], od_pw1_b[0], od_dw_w[0],
                           od_dw_b[0], od_ln_g[0], od_ln_b[0], od_pw2_w[0], od_pw2_b[0],
                           _CastRider(od_moe_gu[0], n_steps, col_blocks=2, col_block=1))
    h = _cross_attn(h, xa_norm[1], xa_wq[1], kv[1], xa_wo[1])
    out = _moe_block(h.reshape(b * s, d), od_norm_moe[0], od_router[0], moe_gate, moe_up,
                     moe_down, final_norm)
    return out.reshape(b, s, d)
```

```python
import functools

import jax
import jax.numpy as jnp
from jax import lax
from jax.experimental import pallas as pl
from jax.experimental.pallas import tpu as pltpu

EPS = 1e-6
BF16 = jnp.bfloat16
F32 = jnp.float32

POOL_WINDOWS = (2, 4, 8, 16)
XATTN_HEADS = 4
N_EXPERTS = 8
LANES = 128
SUBLANES = 8
BF16_TILE_ROWS = 16
VMEM_LIMIT = 56 * 1024 * 1024

SEQ_TILE = 512
MIX_HALO = 8
CONF_HALO = 16
CONV_ROWS = 32
GLU_PITCH = 2
FFN_CHUNK = 256
MOE_ROWS = 512
MOE_CHUNK = 1792
MOE_STEPS = 2
ROW_BATCH = 8
TOK_CHUNK = 512


def _rms(x, g):
    return x * lax.rsqrt(jnp.mean(x * x, axis=-1, keepdims=True) + EPS) * g


def _dot(a, b):
    return jnp.dot(a, b, preferred_element_type=F32)


def _cparams(*sem):
    return pltpu.CompilerParams(dimension_semantics=sem, vmem_limit_bytes=VMEM_LIMIT)


def _const_spec(shape):
    nd = len(shape)
    return pl.BlockSpec(shape, lambda *_: (0,) * nd, pipeline_mode=pl.Buffered(1))


class _CastRider:
    def __init__(self, w, n_steps, col_blocks=1, col_block=0, repeat=1):
        if w.ndim == 2:
            w = w[None]
        e, r, c = w.shape
        per_expert = n_steps // (e * repeat)
        self.w = w
        self.block = (1, r // per_expert, c // col_blocks)
        assert self.block[1] % BF16_TILE_ROWS == 0 and r % per_expert == 0
        self.out_shape = jax.ShapeDtypeStruct((e, r, c // col_blocks), BF16)
        self._per_expert = per_expert
        self._col_block = col_block
        self._repeat = repeat

    def spec(self, step_of):
        pe, cb, rep = self._per_expert, self._col_block, self._repeat

        def in_map(*idx):
            s = step_of(*idx) // rep
            return (s // pe, s % pe, cb)

        def out_map(*idx):
            s = step_of(*idx) // rep
            return (s // pe, s % pe, 0)

        return pl.BlockSpec(self.block, in_map), pl.BlockSpec(self.block, out_map)


def _cast_block(src_ref, dst_ref):
    dst_ref[...] = src_ref[...].astype(BF16)


def _even_mixer_kernel(x_ref, xp_ref, xn_ref, g_ref, win_ref, conv_ref, pw_ref,
                       ps_ref, wout_ref, *rest, seq_len, n_riders):
    cast_src_refs, o_ref = rest[:n_riders], rest[n_riders]
    cast_dst_refs = rest[n_riders + 1:2 * n_riders + 1]
    z_ref, a_ref = rest[2 * n_riders + 1:]
    for src_ref, dst_ref in zip(cast_src_refs, cast_dst_refs):
        _cast_block(src_ref, dst_ref)
    i = pl.program_id(1)
    n_i = pl.num_programs(1)
    ts = x_ref.shape[1]
    h0 = MIX_HALO
    aw = conv_ref.shape[1]
    bg = pw_ref.shape[1]
    xm = x_ref[0]
    xa = jnp.concatenate([xp_ref[0], xm, xn_ref[0]], axis=0)
    hn = _rms(xa, g_ref[...]).astype(BF16)
    z = _dot(hn, win_ref[...])
    row = lax.broadcasted_iota(jnp.int32, (ts + 2 * h0, 1), 0)
    inside = ((row >= h0) | (i > 0)) & ((row < ts + h0) | (i < n_i - 1))
    z_ref[...] = jnp.where(inside, z, 0.0)

    a_ref[...] = z_ref[:, 2 * aw:3 * aw] * z_ref[:, 0:aw]
    cw = conv_ref[...]
    conv = (cw[0:1] * a_ref[h0 - 1:h0 - 1 + ts, :] + cw[1:2] * a_ref[h0:h0 + ts, :]
            + cw[2:3] * a_ref[h0 + 1:h0 + 1 + ts, :])
    parts = [(z_ref[h0:h0 + ts, aw:2 * aw] * conv).astype(BF16)]

    pos = i * ts + lax.broadcasted_iota(jnp.int32, (ts, 1), 0)
    for gi, win in enumerate(POOL_WINDOWS):
        left = win // 2
        right = win - 1 - left
        c0 = 3 * aw + gi * bg
        s = z_ref[h0 - left:h0 - left + ts, c0:c0 + bg]
        for j in range(-left + 1, right + 1):
            s = s + z_ref[h0 + j:h0 + j + ts, c0:c0 + bg]
        cnt = (jnp.minimum(pos + right, seq_len - 1) - jnp.maximum(pos - left, 0) + 1).astype(F32)
        p = s / cnt - z_ref[h0:h0 + ts, c0:c0 + bg]
        yb = _dot(p.astype(BF16), pw_ref[gi]) * ps_ref[:, gi * bg:(gi + 1) * bg]
        parts.append(yb.astype(BF16))
    y = jnp.concatenate(parts, axis=1)
    o_ref[0] = xm + _dot(y, wout_ref[...])


def _even_mixer(x, g, w_in, conv_a, pool_w, pool_scale, w_out, riders):
    b, s, d = x.shape
    ts, h0 = SEQ_TILE, MIX_HALO
    rider_specs = [rd.spec(lambda bi, i: bi * (s // ts) + i) for rd in riders]
    r = ts // h0
    incols = w_in.shape[1]
    aw = conv_a.shape[1]
    h, *casts = pl.pallas_call(
        functools.partial(_even_mixer_kernel, seq_len=s, n_riders=len(riders)),
        grid=(b, s // ts),
        in_specs=[
            pl.BlockSpec((1, ts, d), lambda bi, i: (bi, i, 0)),
            pl.BlockSpec((1, h0, d), lambda bi, i: (bi, jnp.maximum(i * r - 1, 0), 0)),
            pl.BlockSpec((1, h0, d), lambda bi, i: (bi, jnp.minimum((i + 1) * r, s // h0 - 1), 0)),
            _const_spec((1, d)),
            _const_spec((d, incols)),
            _const_spec(conv_a.shape),
            _const_spec(pool_w.shape),
            _const_spec((1, pool_scale.shape[-1])),
            _const_spec(w_out.shape),
        ] + [sp[0] for sp in rider_specs],
        out_specs=[pl.BlockSpec((1, ts, d), lambda bi, i: (bi, i, 0))] + [sp[1] for sp in rider_specs],
        out_shape=[jax.ShapeDtypeStruct((b, s, d), F32)] + [rd.out_shape for rd in riders],
        scratch_shapes=[pltpu.VMEM((ts + 2 * h0, incols), F32),
                        pltpu.VMEM((ts + 2 * h0, aw), F32)],
        compiler_params=_cparams("arbitrary", "arbitrary"),
        name="even_mixer",
    )(x, x, x, g.reshape(1, d), w_in.astype(BF16), conv_a, pool_w.astype(BF16),
      pool_scale.reshape(1, -1), w_out.astype(BF16), *[rd.w for rd in riders])
    return h, casts


def _conformer_kernel(x_ref, xp_ref, xn_ref, g_ref, pw1_ref, b1_ref, dw_ref, dwb_ref,
                      lng_ref, lnb_ref, pw2_ref, b2_ref, cast_src_ref, o_ref, cast_dst_ref,
                      glu_ref, conv_ref):
    _cast_block(cast_src_ref, cast_dst_ref)
    i = pl.program_id(1)
    n_i = pl.num_programs(1)
    ts = x_ref.shape[1]
    d = x_ref.shape[2]
    h0 = CONF_HALO
    taps = dw_ref.shape[0] // SUBLANES
    half = taps // 2
    xm = x_ref[0]
    xa = jnp.concatenate([xp_ref[0], xm, xn_ref[0]], axis=0)
    hn = _rms(xa, g_ref[...]).astype(BF16)
    y1 = _dot(hn, pw1_ref[...]) + b1_ref[...]
    glu = y1[:, :d] * jax.nn.sigmoid(y1[:, d:])
    row = lax.broadcasted_iota(jnp.int32, (ts + 2 * h0, 1), 0)
    inside = ((row >= h0) | (i > 0)) & ((row < ts + h0) | (i < n_i - 1))
    glu = jnp.where(inside, glu, 0.0)
    n_rows = ts + 2 * h0
    for c in range(d // LANES):
        glu_ref[c, pl.ds(0, n_rows, stride=GLU_PITCH), :] = glu[:, c * LANES:(c + 1) * LANES]

    n_sub = CONV_ROWS // SUBLANES
    for c in range(d // LANES):
        cols = slice(c * LANES, (c + 1) * LANES)
        w_taps = [dw_ref[SUBLANES * k:SUBLANES * (k + 1), cols] for k in range(taps)]
        bias = dwb_ref[:, cols]

        def conv_block(rb, carry, c=c, cols=cols, w_taps=w_taps, bias=bias):
            base = pl.multiple_of(rb * CONV_ROWS, CONV_ROWS)
            accs = [jnp.zeros((SUBLANES, LANES), F32) + bias for _ in range(n_sub)]
            for k in range(taps):
                first = base + (h0 - half + k)
                for a in range(n_sub):
                    rows8 = pl.ds((first + a * SUBLANES) * GLU_PITCH, SUBLANES, stride=GLU_PITCH)
                    accs[a] = accs[a] + w_taps[k] * glu_ref[c, rows8, :]
            conv_ref[pl.ds(base, CONV_ROWS), cols] = jnp.concatenate(accs, axis=0)
            return carry

        lax.fori_loop(0, ts // CONV_ROWS, conv_block, 0)

    acc = conv_ref[...]
    mu = jnp.mean(acc, axis=-1, keepdims=True)
    xc = acc - mu
    var = jnp.mean(xc * xc, axis=-1, keepdims=True)
    yn = xc * lax.rsqrt(var + EPS) * lng_ref[...] + lnb_ref[...]
    act = (yn * jax.nn.sigmoid(yn)).astype(BF16)
    o_ref[0] = xm + _dot(act, pw2_ref[...]) + b2_ref[...]


def _conformer(x, g, pw1_w, pw1_b, dw_w, dw_b, ln_g, ln_b, pw2_w, pw2_b, rider):
    b, s, d = x.shape
    ts, h0 = SEQ_TILE, CONF_HALO
    rider_in, rider_out = rider.spec(lambda bi, i: bi * (s // ts) + i)
    r = ts // h0
    return pl.pallas_call(
        _conformer_kernel,
        grid=(b, s // ts),
        in_specs=[
            pl.BlockSpec((1, ts, d), lambda bi, i: (bi, i, 0)),
            pl.BlockSpec((1, h0, d), lambda bi, i: (bi, jnp.maximum(i * r - 1, 0), 0)),
            pl.BlockSpec((1, h0, d), lambda bi, i: (bi, jnp.minimum((i + 1) * r, s // h0 - 1), 0)),
            _const_spec((1, d)),
            _const_spec(pw1_w.shape),
            _const_spec((1, 2 * d)),
            _const_spec((dw_w.shape[0] * SUBLANES, d)),
            _const_spec((1, d)),
            _const_spec((1, d)),
            _const_spec((1, d)),
            _const_spec(pw2_w.shape),
            _const_spec((1, d)),
            rider_in,
        ],
        out_specs=[pl.BlockSpec((1, ts, d), lambda bi, i: (bi, i, 0)), rider_out],
        out_shape=[jax.ShapeDtypeStruct((b, s, d), F32), rider.out_shape],
        scratch_shapes=[pltpu.VMEM((d // LANES, GLU_PITCH * (ts + 2 * h0), LANES), F32),
                        pltpu.VMEM((ts, d), F32)],
        compiler_params=_cparams("parallel", "parallel"),
        name="conformer",
    )(x, x, x, g.reshape(1, d), pw1_w.astype(BF16), pw1_b.reshape(1, -1),
      jnp.repeat(dw_w, SUBLANES, axis=0),
      dw_b.reshape(1, d), ln_g.reshape(1, d), ln_b.reshape(1, d), pw2_w.astype(BF16),
      pw2_b.reshape(1, d), rider.w)


def _kv_kernel(mem_ref, g_ref, wkv_ref, kv_ref):
    mn = _rms(mem_ref[0], g_ref[0]).astype(BF16)
    kv_ref[0, 0] = _dot(mn, wkv_ref[0]).astype(BF16)


def _memory_kv(mem, norm_mem, wkv):
    b, m, d = mem.shape
    nl = wkv.shape[0]
    return pl.pallas_call(
        _kv_kernel,
        grid=(nl, b),
        in_specs=[
            pl.BlockSpec((1, m, d), lambda l, bi: (bi, 0, 0)),
            pl.BlockSpec((1, 1, d), lambda l, bi: (l, 0, 0)),
            pl.BlockSpec((1, d, 2 * d), lambda l, bi: (l, 0, 0)),
        ],
        out_specs=pl.BlockSpec((1, 1, m, 2 * d), lambda l, bi: (l, bi, 0, 0)),
        out_shape=jax.ShapeDtypeStruct((nl, b, m, 2 * d), BF16),
        compiler_params=_cparams("parallel", "parallel"),
        name="memory_kv",
    )(mem, norm_mem.reshape(nl, 1, d), wkv.astype(BF16))


def _xattn_kernel(h_ref, g_ref, wq_ref, kv_ref, wo_ref, o_ref):
    d = h_ref.shape[2]
    hd = d // XATTN_HEADS
    hb = h_ref[0]
    hn = _rms(hb, g_ref[...]).astype(BF16)
    q = (_dot(hn, wq_ref[...]) * (hd ** -0.5)).astype(BF16)
    kv = kv_ref[0]
    outs = []
    for a in range(XATTN_HEADS):
        qh = q[:, a * hd:(a + 1) * hd]
        kh = kv[:, a * hd:(a + 1) * hd]
        vh = kv[:, d + a * hd:d + (a + 1) * hd]
        sc = lax.dot_general(qh, kh, (((1,), (1,)), ((), ())), preferred_element_type=F32)
        e = jnp.exp(sc - jnp.max(sc, axis=-1, keepdims=True))
        o = _dot(e.astype(BF16), vh) / jnp.sum(e, axis=-1, keepdims=True)
        outs.append(o.astype(BF16))
    o_ref[0] = hb + _dot(jnp.concatenate(outs, axis=1), wo_ref[...])


def _cross_attn(h, g, wq, kv, wo):
    b, s, d = h.shape
    ts = SEQ_TILE
    m = kv.shape[1]
    return pl.pallas_call(
        _xattn_kernel,
        grid=(b, s // ts),
        in_specs=[
            pl.BlockSpec((1, ts, d), lambda bi, i: (bi, i, 0)),
            _const_spec((1, d)),
            _const_spec((d, d)),
            pl.BlockSpec((1, m, 2 * d), lambda bi, i: (bi, 0, 0)),
            _const_spec((d, d)),
        ],
        out_specs=pl.BlockSpec((1, ts, d), lambda bi, i: (bi, i, 0)),
        out_shape=jax.ShapeDtypeStruct((b, s, d), F32),
        compiler_params=_cparams("parallel", "parallel"),
        name="cross_attn",
    )(h, g.reshape(1, d), wq.astype(BF16), kv, wo.astype(BF16))


def _swiglu_kernel(h_ref, g_ref, wgu_ref, wd_ref, cast_src_ref, o_ref, cast_dst_ref):
    _cast_block(cast_src_ref, cast_dst_ref)
    ff = wd_ref.shape[0]
    hb = h_ref[...]
    hn = _rms(hb, g_ref[...]).astype(BF16)
    acc = hb
    for c in range(ff // FFN_CHUNK):
        lo = c * FFN_CHUNK
        gate = _dot(hn, wgu_ref[:, lo:lo + FFN_CHUNK])
        up = _dot(hn, wgu_ref[:, ff + lo:ff + lo + FFN_CHUNK])
        act = (gate * jax.nn.sigmoid(gate) * up).astype(BF16)
        acc = acc + _dot(act, wd_ref[lo:lo + FFN_CHUNK, :])
    o_ref[...] = acc


def _dense_swiglu(h2, g, w_gu, w_down, rider):
    t, d = h2.shape
    ts = SEQ_TILE
    rider_in, rider_out = rider.spec(lambda i: i)
    return pl.pallas_call(
        _swiglu_kernel,
        grid=(t // ts,),
        in_specs=[
            pl.BlockSpec((ts, d), lambda i: (i, 0)),
            _const_spec((1, d)),
            _const_spec(w_gu.shape),
            _const_spec(w_down.shape),
            rider_in,
        ],
        out_specs=[pl.BlockSpec((ts, d), lambda i: (i, 0)), rider_out],
        out_shape=[jax.ShapeDtypeStruct((t, d), F32), rider.out_shape],
        compiler_params=_cparams("parallel"),
        name="dense_swiglu",
    )(h2, g.reshape(1, d), w_gu.astype(BF16), w_down.astype(BF16), rider.w)


def _store_row_tiles(ref, x):
    for s in range(SUBLANES):
        ref[pl.ds(s, x.shape[0], stride=SUBLANES), :] = x[:, s * LANES:(s + 1) * LANES]


def _load_row_tiles(ref):
    rows = ref.shape[0] // SUBLANES
    return jnp.concatenate([ref[pl.ds(s, rows, stride=SUBLANES), :] for s in range(SUBLANES)], axis=1)


def _row_tile(ref, r):
    return ref.at[pl.ds(pl.multiple_of(r * SUBLANES, SUBLANES), SUBLANES), :]


def _router_kernel(h_ref, g_ref, wr_ref, hn_ref, meta_ref, total_ref, carry_ref):
    i = pl.program_id(0)
    tc = h_ref.shape[0]

    @pl.when(i == 0)
    def _():
        carry_ref[...] = jnp.zeros_like(carry_ref)

    @pl.when(i == pl.num_programs(0) - 1)
    def _():
        hn_ref[...] = jnp.zeros_like(hn_ref)

    @pl.when(i < pl.num_programs(0) - 1)
    def _():
        _route_chunk(h_ref, g_ref, wr_ref, hn_ref, meta_ref, total_ref, carry_ref)


def _route_chunk(h_ref, g_ref, wr_ref, hn_ref, meta_ref, total_ref, carry_ref):
    tc = h_ref.shape[0]
    hn = _rms(h_ref[...], g_ref[...])
    _store_row_tiles(hn_ref, hn)
    hn_hi = hn.astype(BF16)
    hn_lo = (hn - hn_hi.astype(F32)).astype(BF16)
    logits = (_dot(hn_hi, wr_ref[0]) + _dot(hn_lo, wr_ref[0]) + _dot(hn_hi, wr_ref[1]))
    lane = lax.broadcasted_iota(jnp.int32, (tc, LANES), 1)
    neg = jnp.float32(-jnp.inf)
    l1 = jnp.where(lane < N_EXPERTS, logits, neg)
    m1 = jnp.max(l1, axis=-1, keepdims=True)
    i1 = jnp.min(jnp.where(l1 == m1, lane, LANES), axis=-1, keepdims=True)
    l2 = jnp.where(lane == i1, neg, l1)
    m2 = jnp.max(l2, axis=-1, keepdims=True)
    i2 = jnp.min(jnp.where(l2 == m2, lane, LANES), axis=-1, keepdims=True)
    ex = jnp.exp(m2 - m1)
    w1 = 1.0 / (1.0 + ex)
    w2 = ex / (1.0 + ex)
    sel1 = lane == i1
    sel2 = lane == i2
    onehot = (sel1 | sel2).astype(BF16)
    r = lax.broadcasted_iota(jnp.int32, (tc, tc), 0)
    c = lax.broadcasted_iota(jnp.int32, (tc, tc), 1)
    lower = (c < r).astype(BF16)
    carry = carry_ref[...]
    prior = _dot(lower, onehot) + carry
    rank1 = jnp.sum(jnp.where(sel1, prior, 0.0), axis=-1, keepdims=True)
    rank2 = jnp.sum(jnp.where(sel2, prior, 0.0), axis=-1, keepdims=True)
    meta = jnp.where(lane == 0, i1.astype(F32), 0.0)
    meta = jnp.where(lane == 1, i2.astype(F32), meta)
    meta = jnp.where(lane == 2, rank1, meta)
    meta = jnp.where(lane == 3, rank2, meta)
    meta = jnp.where(lane == 4, w1, meta)
    meta = jnp.where(lane == 5, w2, meta)
    meta_ref[...] = meta
    new_carry = carry + jnp.sum(onehot.astype(F32), axis=0, keepdims=True)
    carry_ref[...] = new_carry
    total_ref[...] = new_carry


def _router(h2, g, router_w):
    t, d = h2.shape
    tc = TOK_CHUNK
    nc = t // tc
    wr = jnp.zeros((d, LANES), F32).at[:, :router_w.shape[1]].set(router_w)
    wr_hi = wr.astype(BF16)
    wr = jnp.stack([wr_hi, (wr - wr_hi.astype(F32)).astype(BF16)])
    last = nc - 1
    return pl.pallas_call(
        _router_kernel,
        grid=(nc + 1,),
        in_specs=[
            pl.BlockSpec((tc, d), lambda i: (jnp.minimum(i, last), 0)),
            _const_spec((1, d)),
            _const_spec((2, d, LANES)),
        ],
        out_specs=[
            pl.BlockSpec((tc * SUBLANES, LANES), lambda i: (i, 0)),
            pl.BlockSpec((tc, LANES), lambda i: (jnp.minimum(i, last), 0)),
            pl.BlockSpec((1, LANES), lambda i: (0, 0)),
        ],
        out_shape=[
            jax.ShapeDtypeStruct(((t + tc) * SUBLANES, LANES), F32),
            jax.ShapeDtypeStruct((t, LANES), F32),
            jax.ShapeDtypeStruct((1, LANES), F32),
        ],
        scratch_shapes=[pltpu.VMEM((1, LANES), F32)],
        compiler_params=_cparams("arbitrary"),
        name="moe_router",
    )(h2, g.reshape(1, d), wr)


def _row_copy(src, dst, sem):
    return pltpu.make_async_copy(src, dst, sem)


def _experts_kernel(te_ref, nt_ref, nreal_ref, inv_cur_ref, inv_nxt_ref, inv_prv_ref, hn_hbm,
                    wg_ref, wu_ref, wd_ref, y_hbm, xin_ref, x_ref, acc_ref, yout_ref,
                    gsem, ssem, *, n_tokens):
    q = pl.program_id(0)
    f = pl.program_id(1)
    nf = pl.num_programs(1)
    nt = nt_ref[0]
    rows = x_ref.shape[0]
    live = q < nt
    slot = q % 2
    other = 1 - slot
    share = -(-rows // MOE_STEPS)
    n_prev = jnp.where(q >= 1, nreal_ref[jnp.maximum(q - 1, 0)], 0)
    n_prev2 = nreal_ref[jnp.maximum(q - 2, 0)]

    def token_row(a):
        return jnp.where(a < 0, n_tokens, jnp.where(a >= n_tokens, a - n_tokens, a))

    def gather_row(src, dst_slot, j):
        _row_copy(_row_tile(hn_hbm, src), _row_tile(xin_ref.at[dst_slot], j), gsem.at[dst_slot]).start()

    def scatter_row(src_slot, j, dst):
        _row_copy(_row_tile(yout_ref.at[src_slot], j), _row_tile(y_hbm, dst), ssem.at[src_slot]).start()

    def wait_gathered(s):
        _row_copy(hn_hbm.at[pl.ds(0, rows * SUBLANES), :], xin_ref.at[s], gsem.at[s]).wait()

    def wait_scattered(s, n):
        size = pl.multiple_of(n * SUBLANES, SUBLANES)
        _row_copy(yout_ref.at[s, pl.ds(0, size), :], y_hbm.at[pl.ds(0, size), :], ssem.at[s]).wait()

    @pl.when((q == 0) & (f == 0))
    def _():
        def one(j, c):
            gather_row(token_row(inv_cur_ref[0, 0, j]), 0, j)
            return c
        lax.fori_loop(0, rows, one, 0, unroll=ROW_BATCH)

    @pl.when((q >= 2) & (q <= nt) & (f == 0))
    def _():
        wait_scattered(slot, n_prev2)

    @pl.when((q <= nt) & (f == 0))
    def _():
        wait_gathered(slot)

    @pl.when(live & (f == 0))
    def _():
        x_ref[...] = _load_row_tiles(xin_ref.at[slot]).astype(BF16)
        acc_ref[...] = jnp.zeros_like(acc_ref)

    @pl.when(live)
    def _():
        lo = f * share

        def issue_batch(b):
            base = lo + b * ROW_BATCH
            srcs = [token_row(inv_nxt_ref[0, 0, base + u]) for u in range(ROW_BATCH)]
            dsts = [inv_prv_ref[0, 0, base + u] for u in range(ROW_BATCH)]
            for u in range(ROW_BATCH):
                gather_row(srcs[u], other, base + u)
            for u in range(ROW_BATCH):
                @pl.when(base + u < n_prev)
                def _(u=u):
                    scatter_row(other, base + u, dsts[u])

        n_sub = wd_ref.shape[1] // FFN_CHUNK
        n_batches = share // ROW_BATCH
        n_issue = max(1, n_sub - 3)
        x = x_ref[...]
        acc = acc_ref[...]
        for c in range(n_sub):
            if c < n_issue:
                for b in range(c * n_batches // n_issue, (c + 1) * n_batches // n_issue):
                    issue_batch(b)
            cols = slice(c * FFN_CHUNK, (c + 1) * FFN_CHUNK)
            gate = _dot(x, wg_ref[0, :, cols])
            up = _dot(x, wu_ref[0, :, cols])
            act = (gate * jax.nn.sigmoid(gate) * up).astype(BF16)
            acc = acc + _dot(act, wd_ref[0, cols, :])
        acc_ref[...] = acc

    @pl.when(live & (f == nf - 1))
    def _():
        _store_row_tiles(yout_ref.at[slot], acc_ref[...])

    @pl.when((q == nt) & (f == 0))
    def _():
        def one(j, c):
            scatter_row(other, j, inv_prv_ref[0, 0, j])
            return c
        lax.fori_loop(0, n_prev, one, 0)
        wait_scattered(other, n_prev)


def _experts(hn_rows, inv, tile_expert, n_tiles, tile_real, w_gate, w_up, w_down, n_tokens):
    d = SUBLANES * LANES
    ne, ff, _ = w_down.shape
    assert w_down.shape[2] == d
    nf = ff // MOE_CHUNK
    assert nf == MOE_STEPS and MOE_ROWS % (MOE_STEPS * ROW_BATCH) == 0
    nq = inv.shape[0]

    def tile(q, nt):
        return jnp.minimum(q, nt[0] - 1)

    def chunk(q, f, nt):
        return jnp.where(q < nt[0], f, nf - 1)

    idx_block = (1, 1, MOE_ROWS)
    return pl.pallas_call(
        functools.partial(_experts_kernel, n_tokens=n_tokens),
        grid_spec=pltpu.PrefetchScalarGridSpec(
            num_scalar_prefetch=3,
            grid=(nq + 1, nf),
            in_specs=[
                pl.BlockSpec(idx_block, lambda q, f, te, nt, nr: (jnp.minimum(q, nq - 1), 0, 0),
                             memory_space=pltpu.SMEM),
                pl.BlockSpec(idx_block, lambda q, f, te, nt, nr: (jnp.minimum(q + 1, nq - 1), 0, 0),
                             memory_space=pltpu.SMEM),
                pl.BlockSpec(idx_block, lambda q, f, te, nt, nr: (jnp.clip(q - 1, 0, nq - 1), 0, 0),
                             memory_space=pltpu.SMEM),
                pl.BlockSpec(memory_space=pl.ANY),
                pl.BlockSpec((1, d, MOE_CHUNK),
                             lambda q, f, te, nt, nr: (te[tile(q, nt)], 0, chunk(q, f, nt))),
                pl.BlockSpec((1, d, MOE_CHUNK),
                             lambda q, f, te, nt, nr: (te[tile(q, nt)], 0, chunk(q, f, nt))),
                pl.BlockSpec((1, MOE_CHUNK, d),
                             lambda q, f, te, nt, nr: (te[tile(q, nt)], chunk(q, f, nt), 0)),
            ],
            out_specs=pl.BlockSpec(memory_space=pl.ANY),
            scratch_shapes=[
                pltpu.VMEM((2, MOE_ROWS * SUBLANES, LANES), F32),
                pltpu.VMEM((MOE_ROWS, d), BF16),
                pltpu.VMEM((MOE_ROWS, d), F32),
                pltpu.VMEM((2, MOE_ROWS * SUBLANES, LANES), F32),
                pltpu.SemaphoreType.DMA((2,)),
                pltpu.SemaphoreType.DMA((2,)),
            ],
        ),
        out_shape=jax.ShapeDtypeStruct((2 * n_tokens * SUBLANES, LANES), F32),
        compiler_params=pltpu.CompilerParams(
            dimension_semantics=("arbitrary", "arbitrary"), vmem_limit_bytes=VMEM_LIMIT,
            disable_bounds_checks=True),
        name="moe_experts",
    )(tile_expert, n_tiles, tile_real, inv, inv, inv, hn_rows, w_gate, w_up, w_down)


def _invert_kernel(dest_ref, unset_hbm, inv_ref, sem):
    fill = pltpu.make_async_copy(unset_hbm, inv_ref, sem)
    fill.start()
    fill.wait()
    batch = 2 * ROW_BATCH

    def put(b, c):
        a0 = b * batch
        rows = [dest_ref[a0 + u] for u in range(batch)]
        for u in range(batch):
            inv_ref[rows[u]] = a0 + u
        return c

    lax.fori_loop(0, dest_ref.shape[0] // batch, put, 0)


def _invert(dest, n_rows):
    assert dest.shape[0] % (2 * ROW_BATCH) == 0
    return pl.pallas_call(
        _invert_kernel,
        in_specs=[pl.BlockSpec(memory_space=pltpu.SMEM), pl.BlockSpec(memory_space=pl.ANY)],
        out_specs=pl.BlockSpec(memory_space=pltpu.SMEM),
        out_shape=jax.ShapeDtypeStruct((n_rows,), jnp.int32),
        scratch_shapes=[pltpu.SemaphoreType.DMA(())],
        name="moe_invert",
    )(dest, jnp.full((n_rows,), -1, jnp.int32))


def _combine_kernel(h_ref, y1_ref, y2_ref, wts_ref, g_ref, o_ref):
    wt = wts_ref[...]
    moe = wt[:, 0:1] * _load_row_tiles(y1_ref) + wt[:, 1:2] * _load_row_tiles(y2_ref)
    o_ref[...] = _rms(h_ref[...] + moe, g_ref[...])


def _combine(h2, y, wts, g_final):
    t, d = h2.shape
    ts = SEQ_TILE
    nt = t // ts
    return pl.pallas_call(
        _combine_kernel,
        grid=(nt,),
        in_specs=[
            pl.BlockSpec((ts, d), lambda i: (i, 0)),
            pl.BlockSpec((ts * SUBLANES, LANES), lambda i: (i, 0)),
            pl.BlockSpec((ts * SUBLANES, LANES), lambda i: (nt + i, 0)),
            pl.BlockSpec((ts, 2), lambda i: (i, 0)),
            _const_spec((1, d)),
        ],
        out_specs=pl.BlockSpec((ts, d), lambda i: (i, 0)),
        out_shape=jax.ShapeDtypeStruct((t, d), F32),
        compiler_params=_cparams("parallel"),
        name="moe_combine",
    )(h2, y, y, wts, g_final.reshape(1, d))


def _moe_block(h2, g, router_w, w_gate, w_up, w_down, g_final):
    t, d = h2.shape
    ne = w_down.shape[0]
    nq = (2 * t) // MOE_ROWS + ne
    n_rows = nq * MOE_ROWS

    hn_rows, meta, total = _router(h2, g, router_w)

    experts = meta[:, 0:2].astype(jnp.int32)
    ranks = meta[:, 2:4].astype(jnp.int32)
    wts = meta[:, 4:6]
    counts = total[0, :ne].astype(jnp.int32)
    tiles_per = (counts + MOE_ROWS - 1) // MOE_ROWS
    tile_end = jnp.cumsum(tiles_per)
    tile_start = tile_end - tiles_per
    n_tiles = tile_end[-1]
    dest = (jnp.take(tile_start * MOE_ROWS, experts) + ranks).T.reshape(-1)
    qs = jnp.arange(nq, dtype=jnp.int32)
    tile_expert = jnp.minimum(jnp.sum((qs[:, None] >= tile_end[None, :]).astype(jnp.int32), axis=1),
                              ne - 1)
    tile_real = jnp.clip(jnp.take(counts, tile_expert) - (qs - jnp.take(tile_start, tile_expert)) * MOE_ROWS,
                         0, MOE_ROWS)
    inv = _invert(dest, n_rows)
    y = _experts(hn_rows, inv.reshape(nq, 1, MOE_ROWS), tile_expert, n_tiles.reshape(1), tile_real,
                 w_gate, w_up, w_down, t)
    return _combine(h2, y, wts, g_final)


def kernel(x, mem, ev_norm_mix, ev_w_in, ev_conv_a, ev_pool_w, ev_pool_scale, ev_w_out, ev_norm_ffn, ev_ffn_gu, ev_ffn_down, od_norm_mix, od_pw1_w, od_pw1_b, od_dw_w, od_dw_b, od_ln_g, od_ln_b, od_pw2_w, od_pw2_b, od_norm_moe, od_router, od_moe_gu, od_moe_down, xa_norm, xa_norm_mem, xa_wq, xa_wkv, xa_wo, final_norm):
    b, s, d = x.shape
    assert ev_w_in.shape[0] == 1 and od_pw1_w.shape[0] == 1 and xa_wq.shape[0] == 2
    kv = _memory_kv(mem, xa_norm_mem, xa_wkv)
    n_steps = (b * s) // SEQ_TILE
    h, (moe_down, ffn_gu, ffn_down, wq, wo, pw1_w, pw2_w) = _even_mixer(
        x, ev_norm_mix[0], ev_w_in[0], ev_conv_a[0], ev_pool_w[0], ev_pool_scale[0], ev_w_out[0],
        [_CastRider(od_moe_down[0], n_steps), _CastRider(ev_ffn_gu[0], n_steps),
         _CastRider(ev_ffn_down[0], n_steps, repeat=2), _CastRider(xa_wq, n_steps),
         _CastRider(xa_wo, n_steps), _CastRider(od_pw1_w[0], n_steps),
         _CastRider(od_pw2_w[0], n_steps)])
    h = _cross_attn(h, xa_norm[0], wq[0], kv[0], wo[0])
    h, moe_gate = _dense_swiglu(h.reshape(b * s, d), ev_norm_ffn[0], ffn_gu[0], ffn_down[0],
                                _CastRider(od_moe_gu[0], n_steps, col_blocks=2, col_block=0))
    h, moe_up = _conformer(h.reshape(b, s, d), od_norm_mix[0], pw1_w[0], od_pw1_b[0], od_dw_w[0],
                           od_dw_b[0], od_ln_g[0], od_ln_b[0], pw2_w[0], od_pw2_b[0],
                           _CastRider(od_moe_gu[0], n_steps, col_blocks=2, col_block=1))
    h = _cross_attn(h, xa_norm[1], wq[1], kv[1], wo[1])
    out = _moe_block(h.reshape(b * s, d), od_norm_moe[0], od_router[0], moe_gate, moe_up,
                     moe_down, final_norm)
    return out.reshape(b, s, d)
```

```python
import functools

import jax
import jax.numpy as jnp
from jax import lax
from jax.experimental import pallas as pl
from jax.experimental.pallas import tpu as pltpu

EPS = 1e-6
BF16 = jnp.bfloat16
F32 = jnp.float32

POOL_WINDOWS = (2, 4, 8, 16)
XATTN_HEADS = 4
N_EXPERTS = 8
LANES = 128
SUBLANES = 8
BF16_TILE_ROWS = 16
VMEM_LIMIT = 56 * 1024 * 1024

SEQ_TILE = 512
MIX_HALO = 8
CONF_HALO = 16
CONV_ROWS = 32
GLU_PITCH = 2
FFN_CHUNK = 256
MOE_ROWS = 512
MOE_CHUNK = 1792
MOE_STEPS = 2
ROW_BATCH = 8
TOK_CHUNK = 512


def _rms(x, g):
    return x * lax.rsqrt(jnp.mean(x * x, axis=-1, keepdims=True) + EPS) * g


def _dot(a, b):
    return jnp.dot(a, b, preferred_element_type=F32)


def _cparams(*sem):
    return pltpu.CompilerParams(dimension_semantics=sem, vmem_limit_bytes=VMEM_LIMIT)


def _const_spec(shape):
    nd = len(shape)
    return pl.BlockSpec(shape, lambda *_: (0,) * nd, pipeline_mode=pl.Buffered(1))


class _CastRider:
    def __init__(self, w, n_steps, col_blocks=1, col_block=0, repeat=1):
        if w.ndim == 2:
            w = w[None]
        e, r, c = w.shape
        per_expert = n_steps // (e * repeat)
        self.w = w
        self.block = (1, r // per_expert, c // col_blocks)
        assert self.block[1] % BF16_TILE_ROWS == 0 and r % per_expert == 0
        self.out_shape = jax.ShapeDtypeStruct((e, r, c // col_blocks), BF16)
        self._per_expert = per_expert
        self._col_block = col_block
        self._repeat = repeat

    def spec(self, step_of):
        pe, cb, rep = self._per_expert, self._col_block, self._repeat

        def in_map(*idx):
            s = step_of(*idx) // rep
            return (s // pe, s % pe, cb)

        def out_map(*idx):
            s = step_of(*idx) // rep
            return (s // pe, s % pe, 0)

        return pl.BlockSpec(self.block, in_map), pl.BlockSpec(self.block, out_map)


def _cast_block(src_ref, dst_ref):
    dst_ref[...] = src_ref[...].astype(BF16)


def _even_mixer_kernel(x_ref, xp_ref, xn_ref, g_ref, win_ref, conv_ref, pw_ref,
                       ps_ref, wout_ref, *rest, seq_len, n_riders):
    cast_src_refs, o_ref = rest[:n_riders], rest[n_riders]
    cast_dst_refs = rest[n_riders + 1:2 * n_riders + 1]
    z_ref, sh_ref = rest[2 * n_riders + 1:]
    for src_ref, dst_ref in zip(cast_src_refs, cast_dst_refs):
        _cast_block(src_ref, dst_ref)
    i = pl.program_id(1)
    n_i = pl.num_programs(1)
    ts = x_ref.shape[1]
    h0 = MIX_HALO
    aw = conv_ref.shape[1]
    bg = pw_ref.shape[1]
    xm = x_ref[0]
    xa = jnp.concatenate([xp_ref[0], xm, xn_ref[0]], axis=0)
    hn = _rms(xa, g_ref[...]).astype(BF16)
    z = _dot(hn, win_ref[...])
    row = lax.broadcasted_iota(jnp.int32, (ts + 2 * h0, 1), 0)
    inside = ((row >= h0) | (i > 0)) & ((row < ts + h0) | (i < n_i - 1))
    z_ref[...] = jnp.where(inside, z, 0.0)

    n_a = aw // LANES
    n_rows = ts + 2 * h0

    def shifted(slab, shift):
        return sh_ref[slab, pl.ds((h0 + shift) * GLU_PITCH, ts, stride=GLU_PITCH), :]

    for c in range(n_a):
        lo = c * LANES
        sh_ref[c, pl.ds(0, n_rows, stride=GLU_PITCH), :] = (z_ref[:, 2 * aw + lo:2 * aw + lo + LANES]
                                                           * z_ref[:, lo:lo + LANES])
    for gi in range(len(POOL_WINDOWS)):
        c0 = 3 * aw + gi * bg
        sh_ref[n_a + gi, pl.ds(0, n_rows, stride=GLU_PITCH), :] = z_ref[:, c0:c0 + bg]

    cw = conv_ref[...]
    conv = jnp.concatenate(
        [cw[0:1, c * LANES:(c + 1) * LANES] * shifted(c, -1)
         + cw[1:2, c * LANES:(c + 1) * LANES] * shifted(c, 0)
         + cw[2:3, c * LANES:(c + 1) * LANES] * shifted(c, 1) for c in range(n_a)], axis=1)
    parts = [(z_ref[h0:h0 + ts, aw:2 * aw] * conv).astype(BF16)]

    pos = i * ts + lax.broadcasted_iota(jnp.int32, (ts, 1), 0)
    for gi, win in enumerate(POOL_WINDOWS):
        left = win // 2
        right = win - 1 - left
        c0 = 3 * aw + gi * bg
        s = shifted(n_a + gi, -left)
        for j in range(-left + 1, right + 1):
            s = s + shifted(n_a + gi, j)
        cnt = (jnp.minimum(pos + right, seq_len - 1) - jnp.maximum(pos - left, 0) + 1).astype(F32)
        p = s / cnt - z_ref[h0:h0 + ts, c0:c0 + bg]
        yb = _dot(p.astype(BF16), pw_ref[gi]) * ps_ref[:, gi * bg:(gi + 1) * bg]
        parts.append(yb.astype(BF16))
    y = jnp.concatenate(parts, axis=1)
    o_ref[0] = xm + _dot(y, wout_ref[...])


def _even_mixer(x, g, w_in, conv_a, pool_w, pool_scale, w_out, riders):
    b, s, d = x.shape
    ts, h0 = SEQ_TILE, MIX_HALO
    rider_specs = [rd.spec(lambda bi, i: bi * (s // ts) + i) for rd in riders]
    r = ts // h0
    incols = w_in.shape[1]
    aw = conv_a.shape[1]
    assert pool_w.shape[1] == LANES and aw % LANES == 0
    h, *casts = pl.pallas_call(
        functools.partial(_even_mixer_kernel, seq_len=s, n_riders=len(riders)),
        grid=(b, s // ts),
        in_specs=[
            pl.BlockSpec((1, ts, d), lambda bi, i: (bi, i, 0)),
            pl.BlockSpec((1, h0, d), lambda bi, i: (bi, jnp.maximum(i * r - 1, 0), 0)),
            pl.BlockSpec((1, h0, d), lambda bi, i: (bi, jnp.minimum((i + 1) * r, s // h0 - 1), 0)),
            _const_spec((1, d)),
            _const_spec((d, incols)),
            _const_spec(conv_a.shape),
            _const_spec(pool_w.shape),
            _const_spec((1, pool_scale.shape[-1])),
            _const_spec(w_out.shape),
        ] + [sp[0] for sp in rider_specs],
        out_specs=[pl.BlockSpec((1, ts, d), lambda bi, i: (bi, i, 0))] + [sp[1] for sp in rider_specs],
        out_shape=[jax.ShapeDtypeStruct((b, s, d), F32)] + [rd.out_shape for rd in riders],
        scratch_shapes=[pltpu.VMEM((ts + 2 * h0, incols), F32),
                        pltpu.VMEM(((incols - 2 * aw) // LANES, GLU_PITCH * (ts + 2 * h0), LANES), F32)],
        compiler_params=_cparams("arbitrary", "arbitrary"),
        name="even_mixer",
    )(x, x, x, g.reshape(1, d), w_in.astype(BF16), conv_a, pool_w.astype(BF16),
      pool_scale.reshape(1, -1), w_out.astype(BF16), *[rd.w for rd in riders])
    return h, casts


def _conformer_kernel(x_ref, xp_ref, xn_ref, g_ref, pw1_ref, b1_ref, dw_ref, dwb_ref,
                      lng_ref, lnb_ref, pw2_ref, b2_ref, cast_src_ref, o_ref, cast_dst_ref,
                      glu_ref, conv_ref):
    _cast_block(cast_src_ref, cast_dst_ref)
    i = pl.program_id(1)
    n_i = pl.num_programs(1)
    ts = x_ref.shape[1]
    d = x_ref.shape[2]
    h0 = CONF_HALO
    taps = dw_ref.shape[0] // SUBLANES
    half = taps // 2
    xm = x_ref[0]
    xa = jnp.concatenate([xp_ref[0], xm, xn_ref[0]], axis=0)
    hn = _rms(xa, g_ref[...]).astype(BF16)
    y1 = _dot(hn, pw1_ref[...]) + b1_ref[...]
    glu = y1[:, :d] * jax.nn.sigmoid(y1[:, d:])
    row = lax.broadcasted_iota(jnp.int32, (ts + 2 * h0, 1), 0)
    inside = ((row >= h0) | (i > 0)) & ((row < ts + h0) | (i < n_i - 1))
    glu = jnp.where(inside, glu, 0.0)
    n_rows = ts + 2 * h0
    for c in range(d // LANES):
        glu_ref[c, pl.ds(0, n_rows, stride=GLU_PITCH), :] = glu[:, c * LANES:(c + 1) * LANES]

    n_sub = CONV_ROWS // SUBLANES
    for c in range(d // LANES):
        cols = slice(c * LANES, (c + 1) * LANES)
        w_taps = [dw_ref[SUBLANES * k:SUBLANES * (k + 1), cols] for k in range(taps)]
        bias = dwb_ref[:, cols]

        def conv_block(rb, carry, c=c, cols=cols, w_taps=w_taps, bias=bias):
            base = pl.multiple_of(rb * CONV_ROWS, CONV_ROWS)
            accs = [jnp.zeros((SUBLANES, LANES), F32) + bias for _ in range(n_sub)]
            for k in range(taps):
                first = base + (h0 - half + k)
                for a in range(n_sub):
                    rows8 = pl.ds((first + a * SUBLANES) * GLU_PITCH, SUBLANES, stride=GLU_PITCH)
                    accs[a] = accs[a] + w_taps[k] * glu_ref[c, rows8, :]
            conv_ref[pl.ds(base, CONV_ROWS), cols] = jnp.concatenate(accs, axis=0)
            return carry

        lax.fori_loop(0, ts // CONV_ROWS, conv_block, 0)

    acc = conv_ref[...]
    mu = jnp.mean(acc, axis=-1, keepdims=True)
    xc = acc - mu
    var = jnp.mean(xc * xc, axis=-1, keepdims=True)
    yn = xc * lax.rsqrt(var + EPS) * lng_ref[...] + lnb_ref[...]
    act = (yn * jax.nn.sigmoid(yn)).astype(BF16)
    o_ref[0] = xm + _dot(act, pw2_ref[...]) + b2_ref[...]


def _conformer(x, g, pw1_w, pw1_b, dw_w, dw_b, ln_g, ln_b, pw2_w, pw2_b, rider):
    b, s, d = x.shape
    ts, h0 = SEQ_TILE, CONF_HALO
    rider_in, rider_out = rider.spec(lambda bi, i: bi * (s // ts) + i)
    r = ts // h0
    return pl.pallas_call(
        _conformer_kernel,
        grid=(b, s // ts),
        in_specs=[
            pl.BlockSpec((1, ts, d), lambda bi, i: (bi, i, 0)),
            pl.BlockSpec((1, h0, d), lambda bi, i: (bi, jnp.maximum(i * r - 1, 0), 0)),
            pl.BlockSpec((1, h0, d), lambda bi, i: (bi, jnp.minimum((i + 1) * r, s // h0 - 1), 0)),
            _const_spec((1, d)),
            _const_spec(pw1_w.shape),
            _const_spec((1, 2 * d)),
            _const_spec((dw_w.shape[0] * SUBLANES, d)),
            _const_spec((1, d)),
            _const_spec((1, d)),
            _const_spec((1, d)),
            _const_spec(pw2_w.shape),
            _const_spec((1, d)),
            rider_in,
        ],
        out_specs=[pl.BlockSpec((1, ts, d), lambda bi, i: (bi, i, 0)), rider_out],
        out_shape=[jax.ShapeDtypeStruct((b, s, d), F32), rider.out_shape],
        scratch_shapes=[pltpu.VMEM((d // LANES, GLU_PITCH * (ts + 2 * h0), LANES), F32),
                        pltpu.VMEM((ts, d), F32)],
        compiler_params=_cparams("parallel", "parallel"),
        name="conformer",
    )(x, x, x, g.reshape(1, d), pw1_w.astype(BF16), pw1_b.reshape(1, -1),
      jnp.repeat(dw_w, SUBLANES, axis=0),
      dw_b.reshape(1, d), ln_g.reshape(1, d), ln_b.reshape(1, d), pw2_w.astype(BF16),
      pw2_b.reshape(1, d), rider.w)


def _kv_kernel(mem_ref, g_ref, wkv_ref, kv_ref):
    mn = _rms(mem_ref[0], g_ref[0]).astype(BF16)
    kv_ref[0, 0] = _dot(mn, wkv_ref[0]).astype(BF16)


def _memory_kv(mem, norm_mem, wkv):
    b, m, d = mem.shape
    nl = wkv.shape[0]
    return pl.pallas_call(
        _kv_kernel,
        grid=(nl, b),
        in_specs=[
            pl.BlockSpec((1, m, d), lambda l, bi: (bi, 0, 0)),
            pl.BlockSpec((1, 1, d), lambda l, bi: (l, 0, 0)),
            pl.BlockSpec((1, d, 2 * d), lambda l, bi: (l, 0, 0)),
        ],
        out_specs=pl.BlockSpec((1, 1, m, 2 * d), lambda l, bi: (l, bi, 0, 0)),
        out_shape=jax.ShapeDtypeStruct((nl, b, m, 2 * d), BF16),
        compiler_params=_cparams("parallel", "parallel"),
        name="memory_kv",
    )(mem, norm_mem.reshape(nl, 1, d), wkv.astype(BF16))


def _xattn_kernel(h_ref, g_ref, wq_ref, kv_ref, wo_ref, o_ref):
    d = h_ref.shape[2]
    hd = d // XATTN_HEADS
    hb = h_ref[0]
    hn = _rms(hb, g_ref[...]).astype(BF16)
    q = (_dot(hn, wq_ref[...]) * (hd ** -0.5)).astype(BF16)
    kv = kv_ref[0]
    outs = []
    for a in range(XATTN_HEADS):
        qh = q[:, a * hd:(a + 1) * hd]
        kh = kv[:, a * hd:(a + 1) * hd]
        vh = kv[:, d + a * hd:d + (a + 1) * hd]
        sc = lax.dot_general(qh, kh, (((1,), (1,)), ((), ())), preferred_element_type=F32)
        e = jnp.exp(sc - jnp.max(sc, axis=-1, keepdims=True))
        o = _dot(e.astype(BF16), vh) / jnp.sum(e, axis=-1, keepdims=True)
        outs.append(o.astype(BF16))
    o_ref[0] = hb + _dot(jnp.concatenate(outs, axis=1), wo_ref[...])


def _cross_attn(h, g, wq, kv, wo):
    b, s, d = h.shape
    ts = SEQ_TILE
    m = kv.shape[1]
    return pl.pallas_call(
        _xattn_kernel,
        grid=(b, s // ts),
        in_specs=[
            pl.BlockSpec((1, ts, d), lambda bi, i: (bi, i, 0)),
            _const_spec((1, d)),
            _const_spec((d, d)),
            pl.BlockSpec((1, m, 2 * d), lambda bi, i: (bi, 0, 0)),
            _const_spec((d, d)),
        ],
        out_specs=pl.BlockSpec((1, ts, d), lambda bi, i: (bi, i, 0)),
        out_shape=jax.ShapeDtypeStruct((b, s, d), F32),
        compiler_params=_cparams("parallel", "parallel"),
        name="cross_attn",
    )(h, g.reshape(1, d), wq.astype(BF16), kv, wo.astype(BF16))


def _swiglu_kernel(h_ref, g_ref, wgu_ref, wd_ref, cast_src_ref, o_ref, cast_dst_ref):
    _cast_block(cast_src_ref, cast_dst_ref)
    ff = wd_ref.shape[0]
    hb = h_ref[...]
    hn = _rms(hb, g_ref[...]).astype(BF16)
    acc = hb
    for c in range(ff // FFN_CHUNK):
        lo = c * FFN_CHUNK
        gate = _dot(hn, wgu_ref[:, lo:lo + FFN_CHUNK])
        up = _dot(hn, wgu_ref[:, ff + lo:ff + lo + FFN_CHUNK])
        act = (gate * jax.nn.sigmoid(gate) * up).astype(BF16)
        acc = acc + _dot(act, wd_ref[lo:lo + FFN_CHUNK, :])
    o_ref[...] = acc


def _dense_swiglu(h2, g, w_gu, w_down, rider):
    t, d = h2.shape
    ts = SEQ_TILE
    rider_in, rider_out = rider.spec(lambda i: i)
    return pl.pallas_call(
        _swiglu_kernel,
        grid=(t // ts,),
        in_specs=[
            pl.BlockSpec((ts, d), lambda i: (i, 0)),
            _const_spec((1, d)),
            _const_spec(w_gu.shape),
            _const_spec(w_down.shape),
            rider_in,
        ],
        out_specs=[pl.BlockSpec((ts, d), lambda i: (i, 0)), rider_out],
        out_shape=[jax.ShapeDtypeStruct((t, d), F32), rider.out_shape],
        compiler_params=_cparams("parallel"),
        name="dense_swiglu",
    )(h2, g.reshape(1, d), w_gu.astype(BF16), w_down.astype(BF16), rider.w)


def _store_row_tiles(ref, x):
    for s in range(SUBLANES):
        ref[pl.ds(s, x.shape[0], stride=SUBLANES), :] = x[:, s * LANES:(s + 1) * LANES]


def _load_row_tiles(ref):
    rows = ref.shape[0] // SUBLANES
    return jnp.concatenate([ref[pl.ds(s, rows, stride=SUBLANES), :] for s in range(SUBLANES)], axis=1)


def _row_tile(ref, r):
    return ref.at[pl.ds(pl.multiple_of(r * SUBLANES, SUBLANES), SUBLANES), :]


def _router_kernel(h_ref, g_ref, wr_ref, hn_ref, meta_ref, total_ref, carry_ref):
    i = pl.program_id(0)
    tc = h_ref.shape[0]

    @pl.when(i == 0)
    def _():
        carry_ref[...] = jnp.zeros_like(carry_ref)

    @pl.when(i == pl.num_programs(0) - 1)
    def _():
        hn_ref[...] = jnp.zeros_like(hn_ref)

    @pl.when(i < pl.num_programs(0) - 1)
    def _():
        _route_chunk(h_ref, g_ref, wr_ref, hn_ref, meta_ref, total_ref, carry_ref)


def _route_chunk(h_ref, g_ref, wr_ref, hn_ref, meta_ref, total_ref, carry_ref):
    tc = h_ref.shape[0]
    hn = _rms(h_ref[...], g_ref[...])
    _store_row_tiles(hn_ref, hn)
    hn_hi = hn.astype(BF16)
    hn_lo = (hn - hn_hi.astype(F32)).astype(BF16)
    logits = (_dot(hn_hi, wr_ref[0]) + _dot(hn_lo, wr_ref[0]) + _dot(hn_hi, wr_ref[1]))
    lane = lax.broadcasted_iota(jnp.int32, (tc, LANES), 1)
    neg = jnp.float32(-jnp.inf)
    l1 = jnp.where(lane < N_EXPERTS, logits, neg)
    m1 = jnp.max(l1, axis=-1, keepdims=True)
    i1 = jnp.min(jnp.where(l1 == m1, lane, LANES), axis=-1, keepdims=True)
    l2 = jnp.where(lane == i1, neg, l1)
    m2 = jnp.max(l2, axis=-1, keepdims=True)
    i2 = jnp.min(jnp.where(l2 == m2, lane, LANES), axis=-1, keepdims=True)
    ex = jnp.exp(m2 - m1)
    w1 = 1.0 / (1.0 + ex)
    w2 = ex / (1.0 + ex)
    sel1 = lane == i1
    sel2 = lane == i2
    onehot = (sel1 | sel2).astype(BF16)
    r = lax.broadcasted_iota(jnp.int32, (tc, tc), 0)
    c = lax.broadcasted_iota(jnp.int32, (tc, tc), 1)
    lower = (c < r).astype(BF16)
    carry = carry_ref[...]
    prior = _dot(lower, onehot) + carry
    rank1 = jnp.sum(jnp.where(sel1, prior, 0.0), axis=-1, keepdims=True)
    rank2 = jnp.sum(jnp.where(sel2, prior, 0.0), axis=-1, keepdims=True)
    meta = jnp.where(lane == 0, i1.astype(F32), 0.0)
    meta = jnp.where(lane == 1, i2.astype(F32), meta)
    meta = jnp.where(lane == 2, rank1, meta)
    meta = jnp.where(lane == 3, rank2, meta)
    meta = jnp.where(lane == 4, w1, meta)
    meta = jnp.where(lane == 5, w2, meta)
    meta_ref[...] = meta
    new_carry = carry + jnp.sum(onehot.astype(F32), axis=0, keepdims=True)
    carry_ref[...] = new_carry
    total_ref[...] = new_carry


def _router(h2, g, router_w):
    t, d = h2.shape
    tc = TOK_CHUNK
    nc = t // tc
    wr = jnp.zeros((d, LANES), F32).at[:, :router_w.shape[1]].set(router_w)
    wr_hi = wr.astype(BF16)
    wr = jnp.stack([wr_hi, (wr - wr_hi.astype(F32)).astype(BF16)])
    last = nc - 1
    return pl.pallas_call(
        _router_kernel,
        grid=(nc + 1,),
        in_specs=[
            pl.BlockSpec((tc, d), lambda i: (jnp.minimum(i, last), 0)),
            _const_spec((1, d)),
            _const_spec((2, d, LANES)),
        ],
        out_specs=[
            pl.BlockSpec((tc * SUBLANES, LANES), lambda i: (i, 0)),
            pl.BlockSpec((tc, LANES), lambda i: (jnp.minimum(i, last), 0)),
            pl.BlockSpec((1, LANES), lambda i: (0, 0)),
        ],
        out_shape=[
            jax.ShapeDtypeStruct(((t + tc) * SUBLANES, LANES), F32),
            jax.ShapeDtypeStruct((t, LANES), F32),
            jax.ShapeDtypeStruct((1, LANES), F32),
        ],
        scratch_shapes=[pltpu.VMEM((1, LANES), F32)],
        compiler_params=_cparams("arbitrary"),
        name="moe_router",
    )(h2, g.reshape(1, d), wr)


def _row_copy(src, dst, sem):
    return pltpu.make_async_copy(src, dst, sem)


def _experts_kernel(te_ref, nt_ref, nreal_ref, inv_cur_ref, inv_nxt_ref, inv_prv_ref, hn_hbm,
                    wg_ref, wu_ref, wd_ref, y_hbm, xin_ref, x_ref, acc_ref, yout_ref,
                    gsem, ssem, *, n_tokens):
    q = pl.program_id(0)
    f = pl.program_id(1)
    nf = pl.num_programs(1)
    nt = nt_ref[0]
    rows = x_ref.shape[0]
    live = q < nt
    slot = q % 2
    other = 1 - slot
    share = -(-rows // MOE_STEPS)
    n_prev = jnp.where(q >= 1, nreal_ref[jnp.maximum(q - 1, 0)], 0)
    n_prev2 = nreal_ref[jnp.maximum(q - 2, 0)]

    def token_row(a):
        return jnp.where(a < 0, n_tokens, jnp.where(a >= n_tokens, a - n_tokens, a))

    def gather_row(src, dst_slot, j):
        _row_copy(_row_tile(hn_hbm, src), _row_tile(xin_ref.at[dst_slot], j), gsem.at[dst_slot]).start()

    def scatter_row(src_slot, j, dst):
        _row_copy(_row_tile(yout_ref.at[src_slot], j), _row_tile(y_hbm, dst), ssem.at[src_slot]).start()

    def wait_gathered(s):
        _row_copy(hn_hbm.at[pl.ds(0, rows * SUBLANES), :], xin_ref.at[s], gsem.at[s]).wait()

    def wait_scattered(s, n):
        size = pl.multiple_of(n * SUBLANES, SUBLANES)
        _row_copy(yout_ref.at[s, pl.ds(0, size), :], y_hbm.at[pl.ds(0, size), :], ssem.at[s]).wait()

    @pl.when((q == 0) & (f == 0))
    def _():
        def one(j, c):
            gather_row(token_row(inv_cur_ref[0, 0, j]), 0, j)
            return c
        lax.fori_loop(0, rows, one, 0, unroll=ROW_BATCH)

    @pl.when((q >= 2) & (q <= nt) & (f == 0))
    def _():
        wait_scattered(slot, n_prev2)

    @pl.when((q <= nt) & (f == 0))
    def _():
        wait_gathered(slot)

    @pl.when(live & (f == 0))
    def _():
        x_ref[...] = _load_row_tiles(xin_ref.at[slot]).astype(BF16)
        acc_ref[...] = jnp.zeros_like(acc_ref)

    @pl.when(live)
    def _():
        lo = f * share

        def issue_batch(b):
            base = lo + b * ROW_BATCH
            srcs = [token_row(inv_nxt_ref[0, 0, base + u]) for u in range(ROW_BATCH)]
            dsts = [inv_prv_ref[0, 0, base + u] for u in range(ROW_BATCH)]
            for u in range(ROW_BATCH):
                gather_row(srcs[u], other, base + u)
            for u in range(ROW_BATCH):
                @pl.when(base + u < n_prev)
                def _(u=u):
                    scatter_row(other, base + u, dsts[u])

        n_sub = wd_ref.shape[1] // FFN_CHUNK
        n_batches = share // ROW_BATCH
        n_issue = max(1, n_sub - 3)
        x = x_ref[...]
        acc = acc_ref[...]
        for c in range(n_sub):
            if c < n_issue:
                for b in range(c * n_batches // n_issue, (c + 1) * n_batches // n_issue):
                    issue_batch(b)
            cols = slice(c * FFN_CHUNK, (c + 1) * FFN_CHUNK)
            gate = _dot(x, wg_ref[0, :, cols])
            up = _dot(x, wu_ref[0, :, cols])
            act = (gate * jax.nn.sigmoid(gate) * up).astype(BF16)
            acc = acc + _dot(act, wd_ref[0, cols, :])
        acc_ref[...] = acc

    @pl.when(live & (f == nf - 1))
    def _():
        _store_row_tiles(yout_ref.at[slot], acc_ref[...])

    @pl.when((q == nt) & (f == 0))
    def _():
        def one(j, c):
            scatter_row(other, j, inv_prv_ref[0, 0, j])
            return c
        lax.fori_loop(0, n_prev, one, 0)
        wait_scattered(other, n_prev)


def _experts(hn_rows, inv, tile_expert, n_tiles, tile_real, w_gate, w_up, w_down, n_tokens):
    d = SUBLANES * LANES
    ne, ff, _ = w_down.shape
    assert w_down.shape[2] == d
    nf = ff // MOE_CHUNK
    assert nf == MOE_STEPS and MOE_ROWS % (MOE_STEPS * ROW_BATCH) == 0
    nq = inv.shape[0]

    def tile(q, nt):
        return jnp.minimum(q, nt[0] - 1)

    def chunk(q, f, nt):
        return jnp.where(q < nt[0], f, nf - 1)

    idx_block = (1, 1, MOE_ROWS)
    return pl.pallas_call(
        functools.partial(_experts_kernel, n_tokens=n_tokens),
        grid_spec=pltpu.PrefetchScalarGridSpec(
            num_scalar_prefetch=3,
            grid=(nq + 1, nf),
            in_specs=[
                pl.BlockSpec(idx_block, lambda q, f, te, nt, nr: (jnp.minimum(q, nq - 1), 0, 0),
                             memory_space=pltpu.SMEM),
                pl.BlockSpec(idx_block, lambda q, f, te, nt, nr: (jnp.minimum(q + 1, nq - 1), 0, 0),
                             memory_space=pltpu.SMEM),
                pl.BlockSpec(idx_block, lambda q, f, te, nt, nr: (jnp.clip(q - 1, 0, nq - 1), 0, 0),
                             memory_space=pltpu.SMEM),
                pl.BlockSpec(memory_space=pl.ANY),
                pl.BlockSpec((1, d, MOE_CHUNK),
                             lambda q, f, te, nt, nr: (te[tile(q, nt)], 0, chunk(q, f, nt))),
                pl.BlockSpec((1, d, MOE_CHUNK),
                             lambda q, f, te, nt, nr: (te[tile(q, nt)], 0, chunk(q, f, nt))),
                pl.BlockSpec((1, MOE_CHUNK, d),
                             lambda q, f, te, nt, nr: (te[tile(q, nt)], chunk(q, f, nt), 0)),
            ],
            out_specs=pl.BlockSpec(memory_space=pl.ANY),
            scratch_shapes=[
                pltpu.VMEM((2, MOE_ROWS * SUBLANES, LANES), F32),
                pltpu.VMEM((MOE_ROWS, d), BF16),
                pltpu.VMEM((MOE_ROWS, d), F32),
                pltpu.VMEM((2, MOE_ROWS * SUBLANES, LANES), F32),
                pltpu.SemaphoreType.DMA((2,)),
                pltpu.SemaphoreType.DMA((2,)),
            ],
        ),
        out_shape=jax.ShapeDtypeStruct((2 * n_tokens * SUBLANES, LANES), F32),
        compiler_params=pltpu.CompilerParams(
            dimension_semantics=("arbitrary", "arbitrary"), vmem_limit_bytes=VMEM_LIMIT,
            disable_bounds_checks=True),
        name="moe_experts",
    )(tile_expert, n_tiles, tile_real, inv, inv, inv, hn_rows, w_gate, w_up, w_down)


def _invert_kernel(dest_ref, unset_hbm, inv_ref, sem):
    fill = pltpu.make_async_copy(unset_hbm, inv_ref, sem)
    fill.start()
    fill.wait()
    batch = 2 * ROW_BATCH

    def put(b, c):
        a0 = b * batch
        rows = [dest_ref[a0 + u] for u in range(batch)]
        for u in range(batch):
            inv_ref[rows[u]] = a0 + u
        return c

    lax.fori_loop(0, dest_ref.shape[0] // batch, put, 0)


def _invert(dest, n_rows):
    assert dest.shape[0] % (2 * ROW_BATCH) == 0
    return pl.pallas_call(
        _invert_kernel,
        in_specs=[pl.BlockSpec(memory_space=pltpu.SMEM), pl.BlockSpec(memory_space=pl.ANY)],
        out_specs=pl.BlockSpec(memory_space=pltpu.SMEM),
        out_shape=jax.ShapeDtypeStruct((n_rows,), jnp.int32),
        scratch_shapes=[pltpu.SemaphoreType.DMA(())],
        name="moe_invert",
    )(dest, jnp.full((n_rows,), -1, jnp.int32))


def _combine_kernel(h_ref, y1_ref, y2_ref, wts_ref, g_ref, o_ref):
    wt = wts_ref[...]
    moe = wt[:, 0:1] * _load_row_tiles(y1_ref) + wt[:, 1:2] * _load_row_tiles(y2_ref)
    o_ref[...] = _rms(h_ref[...] + moe, g_ref[...])


def _combine(h2, y, wts, g_final):
    t, d = h2.shape
    ts = SEQ_TILE
    nt = t // ts
    return pl.pallas_call(
        _combine_kernel,
        grid=(nt,),
        in_specs=[
            pl.BlockSpec((ts, d), lambda i: (i, 0)),
            pl.BlockSpec((ts * SUBLANES, LANES), lambda i: (i, 0)),
            pl.BlockSpec((ts * SUBLANES, LANES), lambda i: (nt + i, 0)),
            pl.BlockSpec((ts, 2), lambda i: (i, 0)),
            _const_spec((1, d)),
        ],
        out_specs=pl.BlockSpec((ts, d), lambda i: (i, 0)),
        out_shape=jax.ShapeDtypeStruct((t, d), F32),
        compiler_params=_cparams("parallel"),
        name="moe_combine",
    )(h2, y, y, wts, g_final.reshape(1, d))


def _moe_block(h2, g, router_w, w_gate, w_up, w_down, g_final):
    t, d = h2.shape
    ne = w_down.shape[0]
    nq = (2 * t) // MOE_ROWS + ne
    n_rows = nq * MOE_ROWS

    hn_rows, meta, total = _router(h2, g, router_w)

    experts = meta[:, 0:2].astype(jnp.int32)
    ranks = meta[:, 2:4].astype(jnp.int32)
    wts = meta[:, 4:6]
    counts = total[0, :ne].astype(jnp.int32)
    tiles_per = (counts + MOE_ROWS - 1) // MOE_ROWS
    tile_end = jnp.cumsum(tiles_per)
    tile_start = tile_end - tiles_per
    n_tiles = tile_end[-1]
    dest = (jnp.take(tile_start * MOE_ROWS, experts) + ranks).T.reshape(-1)
    qs = jnp.arange(nq, dtype=jnp.int32)
    tile_expert = jnp.minimum(jnp.sum((qs[:, None] >= tile_end[None, :]).astype(jnp.int32), axis=1),
                              ne - 1)
    tile_real = jnp.clip(jnp.take(counts, tile_expert) - (qs - jnp.take(tile_start, tile_expert)) * MOE_ROWS,
                         0, MOE_ROWS)
    inv = _invert(dest, n_rows)
    y = _experts(hn_rows, inv.reshape(nq, 1, MOE_ROWS), tile_expert, n_tiles.reshape(1), tile_real,
                 w_gate, w_up, w_down, t)
    return _combine(h2, y, wts, g_final)


def kernel(x, mem, ev_norm_mix, ev_w_in, ev_conv_a, ev_pool_w, ev_pool_scale, ev_w_out, ev_norm_ffn, ev_ffn_gu, ev_ffn_down, od_norm_mix, od_pw1_w, od_pw1_b, od_dw_w, od_dw_b, od_ln_g, od_ln_b, od_pw2_w, od_pw2_b, od_norm_moe, od_router, od_moe_gu, od_moe_down, xa_norm, xa_norm_mem, xa_wq, xa_wkv, xa_wo, final_norm):
    b, s, d = x.shape
    assert ev_w_in.shape[0] == 1 and od_pw1_w.shape[0] == 1 and xa_wq.shape[0] == 2
    kv = _memory_kv(mem, xa_norm_mem, xa_wkv)
    n_steps = (b * s) // SEQ_TILE
    h, (moe_down, ffn_gu, ffn_down, wq, wo, pw1_w, pw2_w) = _even_mixer(
        x, ev_norm_mix[0], ev_w_in[0], ev_conv_a[0], ev_pool_w[0], ev_pool_scale[0], ev_w_out[0],
        [_CastRider(od_moe_down[0], n_steps), _CastRider(ev_ffn_gu[0], n_steps),
         _CastRider(ev_ffn_down[0], n_steps, repeat=2), _CastRider(xa_wq, n_steps),
         _CastRider(xa_wo, n_steps), _CastRider(od_pw1_w[0], n_steps),
         _CastRider(od_pw2_w[0], n_steps)])
    h = _cross_attn(h, xa_norm[0], wq[0], kv[0], wo[0])
    h, moe_gate = _dense_swiglu(h.reshape(b * s, d), ev_norm_ffn[0], ffn_gu[0], ffn_down[0],
                                _CastRider(od_moe_gu[0], n_steps, col_blocks=2, col_block=0))
    h, moe_up = _conformer(h.reshape(b, s, d), od_norm_mix[0], pw1_w[0], od_pw1_b[0], od_dw_w[0],
                           od_dw_b[0], od_ln_g[0], od_ln_b[0], pw2_w[0], od_pw2_b[0],
                           _CastRider(od_moe_gu[0], n_steps, col_blocks=2, col_block=1))
    h = _cross_attn(h, xa_norm[1], wq[1], kv[1], wo[1])
    out = _moe_block(h.reshape(b * s, d), od_norm_moe[0], od_router[0], moe_gate, moe_up,
                     moe_down, final_norm)
    return out.reshape(b, s, d)
```

```python
import functools

import jax
import jax.numpy as jnp
from jax import lax
from jax.experimental import pallas as pl
from jax.experimental.pallas import tpu as pltpu

EPS = 1e-6
BF16 = jnp.bfloat16
F32 = jnp.float32

POOL_WINDOWS = (2, 4, 8, 16)
XATTN_HEADS = 4
N_EXPERTS = 8
LANES = 128
SUBLANES = 8
BF16_TILE_ROWS = 16
VMEM_LIMIT = 56 * 1024 * 1024

SEQ_TILE = 512
MIX_HALO = 8
CONF_HALO = 16
CONV_ROWS = 32
GLU_PITCH = 2
FFN_CHUNK = 256
MOE_ROWS = 512
MOE_CHUNK = 1792
MOE_STEPS = 2
ROW_BATCH = 8
TOK_CHUNK = 512


def _rms(x, g):
    return x * lax.rsqrt(jnp.mean(x * x, axis=-1, keepdims=True) + EPS) * g


def _dot(a, b):
    return jnp.dot(a, b, preferred_element_type=F32)


def _cparams(*sem):
    return pltpu.CompilerParams(dimension_semantics=sem, vmem_limit_bytes=VMEM_LIMIT)


def _const_spec(shape):
    nd = len(shape)
    return pl.BlockSpec(shape, lambda *_: (0,) * nd, pipeline_mode=pl.Buffered(1))


class _CastRider:
    def __init__(self, w, n_steps, col_blocks=1, col_block=0, repeat=1):
        if w.ndim == 2:
            w = w[None]
        e, r, c = w.shape
        per_expert = n_steps // (e * repeat)
        self.w = w
        self.block = (1, r // per_expert, c // col_blocks)
        assert self.block[1] % BF16_TILE_ROWS == 0 and r % per_expert == 0
        self.out_shape = jax.ShapeDtypeStruct((e, r, c // col_blocks), BF16)
        self._per_expert = per_expert
        self._col_block = col_block
        self._repeat = repeat

    def spec(self, step_of):
        pe, cb, rep = self._per_expert, self._col_block, self._repeat

        def in_map(*idx):
            s = step_of(*idx) // rep
            return (s // pe, s % pe, cb)

        def out_map(*idx):
            s = step_of(*idx) // rep
            return (s // pe, s % pe, 0)

        return pl.BlockSpec(self.block, in_map), pl.BlockSpec(self.block, out_map)


def _cast_block(src_ref, dst_ref):
    dst_ref[...] = src_ref[...].astype(BF16)


def _even_mixer_kernel(x_ref, xp_ref, xn_ref, g_ref, win_ref, conv_ref, pw_ref,
                       ps_ref, wout_ref, *rest, seq_len, n_riders):
    cast_src_refs, o_ref = rest[:n_riders], rest[n_riders]
    cast_dst_refs = rest[n_riders + 1:2 * n_riders + 1]
    z_ref, sh_ref = rest[2 * n_riders + 1:]
    for src_ref, dst_ref in zip(cast_src_refs, cast_dst_refs):
        _cast_block(src_ref, dst_ref)
    i = pl.program_id(1)
    n_i = pl.num_programs(1)
    ts = x_ref.shape[1]
    h0 = MIX_HALO
    aw = conv_ref.shape[1]
    bg = pw_ref.shape[1]
    xm = x_ref[0]
    xa = jnp.concatenate([xp_ref[0], xm, xn_ref[0]], axis=0)
    hn = _rms(xa, g_ref[...]).astype(BF16)
    z = _dot(hn, win_ref[...])
    row = lax.broadcasted_iota(jnp.int32, (ts + 2 * h0, 1), 0)
    inside = ((row >= h0) | (i > 0)) & ((row < ts + h0) | (i < n_i - 1))
    z_ref[...] = jnp.where(inside, z, 0.0)

    n_a = aw // LANES
    n_rows = ts + 2 * h0

    def shifted(slab, shift):
        return sh_ref[slab, pl.ds((h0 + shift) * GLU_PITCH, ts, stride=GLU_PITCH), :]

    for c in range(n_a):
        lo = c * LANES
        sh_ref[c, pl.ds(0, n_rows, stride=GLU_PITCH), :] = (z_ref[:, 2 * aw + lo:2 * aw + lo + LANES]
                                                           * z_ref[:, lo:lo + LANES])
    for gi in range(len(POOL_WINDOWS)):
        c0 = 3 * aw + gi * bg
        sh_ref[n_a + gi, pl.ds(0, n_rows, stride=GLU_PITCH), :] = z_ref[:, c0:c0 + bg]

    cw = conv_ref[...]
    conv = jnp.concatenate(
        [cw[0:1, c * LANES:(c + 1) * LANES] * shifted(c, -1)
         + cw[1:2, c * LANES:(c + 1) * LANES] * shifted(c, 0)
         + cw[2:3, c * LANES:(c + 1) * LANES] * shifted(c, 1) for c in range(n_a)], axis=1)
    parts = [(z_ref[h0:h0 + ts, aw:2 * aw] * conv).astype(BF16)]

    pos = i * ts + lax.broadcasted_iota(jnp.int32, (ts, 1), 0)
    for gi, win in enumerate(POOL_WINDOWS):
        left = win // 2
        right = win - 1 - left
        c0 = 3 * aw + gi * bg
        s = shifted(n_a + gi, -left)
        for j in range(-left + 1, right + 1):
            s = s + shifted(n_a + gi, j)
        cnt = (jnp.minimum(pos + right, seq_len - 1) - jnp.maximum(pos - left, 0) + 1).astype(F32)
        p = s / cnt - z_ref[h0:h0 + ts, c0:c0 + bg]
        yb = _dot(p.astype(BF16), pw_ref[gi]) * ps_ref[:, gi * bg:(gi + 1) * bg]
        parts.append(yb.astype(BF16))
    y = jnp.concatenate(parts, axis=1)
    o_ref[0] = xm + _dot(y, wout_ref[...])


def _even_mixer(x, g, w_in, conv_a, pool_w, pool_scale, w_out, riders):
    b, s, d = x.shape
    ts, h0 = SEQ_TILE, MIX_HALO
    rider_specs = [rd.spec(lambda bi, i: bi * (s // ts) + i) for rd in riders]
    r = ts // h0
    incols = w_in.shape[1]
    aw = conv_a.shape[1]
    assert pool_w.shape[1] == LANES and aw % LANES == 0
    h, *casts = pl.pallas_call(
        functools.partial(_even_mixer_kernel, seq_len=s, n_riders=len(riders)),
        grid=(b, s // ts),
        in_specs=[
            pl.BlockSpec((1, ts, d), lambda bi, i: (bi, i, 0)),
            pl.BlockSpec((1, h0, d), lambda bi, i: (bi, jnp.maximum(i * r - 1, 0), 0)),
            pl.BlockSpec((1, h0, d), lambda bi, i: (bi, jnp.minimum((i + 1) * r, s // h0 - 1), 0)),
            _const_spec((1, d)),
            _const_spec((d, incols)),
            _const_spec(conv_a.shape),
            _const_spec(pool_w.shape),
            _const_spec((1, pool_scale.shape[-1])),
            _const_spec(w_out.shape),
        ] + [sp[0] for sp in rider_specs],
        out_specs=[pl.BlockSpec((1, ts, d), lambda bi, i: (bi, i, 0))] + [sp[1] for sp in rider_specs],
        out_shape=[jax.ShapeDtypeStruct((b, s, d), F32)] + [rd.out_shape for rd in riders],
        scratch_shapes=[pltpu.VMEM((ts + 2 * h0, incols), F32),
                        pltpu.VMEM(((incols - 2 * aw) // LANES, GLU_PITCH * (ts + 2 * h0), LANES), F32)],
        compiler_params=_cparams("arbitrary", "arbitrary"),
        name="even_mixer",
    )(x, x, x, g.reshape(1, d), w_in.astype(BF16), conv_a, pool_w.astype(BF16),
      pool_scale.reshape(1, -1), w_out.astype(BF16), *[rd.w for rd in riders])
    return h, casts


def _conformer_kernel(x_ref, xp_ref, xn_ref, g_ref, pw1_ref, b1_ref, dw_ref, dwb_ref,
                      lng_ref, lnb_ref, pw2_ref, b2_ref, cast_src_ref, o_ref, cast_dst_ref,
                      glu_ref, conv_ref):
    _cast_block(cast_src_ref, cast_dst_ref)
    i = pl.program_id(1)
    n_i = pl.num_programs(1)
    ts = x_ref.shape[1]
    d = x_ref.shape[2]
    h0 = CONF_HALO
    taps = dw_ref.shape[0] // SUBLANES
    half = taps // 2
    xm = x_ref[0]
    xa = jnp.concatenate([xp_ref[0], xm, xn_ref[0]], axis=0)
    hn = _rms(xa, g_ref[...]).astype(BF16)
    y1 = _dot(hn, pw1_ref[...]) + b1_ref[...]
    glu = y1[:, :d] * jax.nn.sigmoid(y1[:, d:])
    row = lax.broadcasted_iota(jnp.int32, (ts + 2 * h0, 1), 0)
    inside = ((row >= h0) | (i > 0)) & ((row < ts + h0) | (i < n_i - 1))
    glu = jnp.where(inside, glu, 0.0)
    n_rows = ts + 2 * h0
    for c in range(d // LANES):
        glu_ref[c, pl.ds(0, n_rows, stride=GLU_PITCH), :] = glu[:, c * LANES:(c + 1) * LANES]

    n_sub = CONV_ROWS // SUBLANES
    for c in range(d // LANES):
        cols = slice(c * LANES, (c + 1) * LANES)
        w_taps = [dw_ref[SUBLANES * k:SUBLANES * (k + 1), cols] for k in range(taps)]
        bias = dwb_ref[:, cols]

        def conv_block(rb, carry, c=c, cols=cols, w_taps=w_taps, bias=bias):
            base = pl.multiple_of(rb * CONV_ROWS, CONV_ROWS)
            accs = [jnp.zeros((SUBLANES, LANES), F32) + bias for _ in range(n_sub)]
            for k in range(taps):
                first = base + (h0 - half + k)
                for a in range(n_sub):
                    rows8 = pl.ds((first + a * SUBLANES) * GLU_PITCH, SUBLANES, stride=GLU_PITCH)
                    accs[a] = accs[a] + w_taps[k] * glu_ref[c, rows8, :]
            conv_ref[pl.ds(base, CONV_ROWS), cols] = jnp.concatenate(accs, axis=0)
            return carry

        lax.fori_loop(0, ts // CONV_ROWS, conv_block, 0)

    acc = conv_ref[...]
    mu = jnp.mean(acc, axis=-1, keepdims=True)
    xc = acc - mu
    var = jnp.mean(xc * xc, axis=-1, keepdims=True)
    yn = xc * lax.rsqrt(var + EPS) * lng_ref[...] + lnb_ref[...]
    act = (yn * jax.nn.sigmoid(yn)).astype(BF16)
    o_ref[0] = xm + _dot(act, pw2_ref[...]) + b2_ref[...]


def _conformer(x, g, pw1_w, pw1_b, dw_w, dw_b, ln_g, ln_b, pw2_w, pw2_b, rider):
    b, s, d = x.shape
    ts, h0 = SEQ_TILE, CONF_HALO
    rider_in, rider_out = rider.spec(lambda bi, i: bi * (s // ts) + i)
    r = ts // h0
    return pl.pallas_call(
        _conformer_kernel,
        grid=(b, s // ts),
        in_specs=[
            pl.BlockSpec((1, ts, d), lambda bi, i: (bi, i, 0)),
            pl.BlockSpec((1, h0, d), lambda bi, i: (bi, jnp.maximum(i * r - 1, 0), 0)),
            pl.BlockSpec((1, h0, d), lambda bi, i: (bi, jnp.minimum((i + 1) * r, s // h0 - 1), 0)),
            _const_spec((1, d)),
            _const_spec(pw1_w.shape),
            _const_spec((1, 2 * d)),
            _const_spec((dw_w.shape[0] * SUBLANES, d)),
            _const_spec((1, d)),
            _const_spec((1, d)),
            _const_spec((1, d)),
            _const_spec(pw2_w.shape),
            _const_spec((1, d)),
            rider_in,
        ],
        out_specs=[pl.BlockSpec((1, ts, d), lambda bi, i: (bi, i, 0)), rider_out],
        out_shape=[jax.ShapeDtypeStruct((b, s, d), F32), rider.out_shape],
        scratch_shapes=[pltpu.VMEM((d // LANES, GLU_PITCH * (ts + 2 * h0), LANES), F32),
                        pltpu.VMEM((ts, d), F32)],
        compiler_params=_cparams("parallel", "parallel"),
        name="conformer",
    )(x, x, x, g.reshape(1, d), pw1_w.astype(BF16), pw1_b.reshape(1, -1),
      jnp.repeat(dw_w, SUBLANES, axis=0),
      dw_b.reshape(1, d), ln_g.reshape(1, d), ln_b.reshape(1, d), pw2_w.astype(BF16),
      pw2_b.reshape(1, d), rider.w)


def _kv_kernel(mem_ref, g_ref, wkv_ref, kv_ref):
    mn = _rms(mem_ref[0], g_ref[0]).astype(BF16)
    kv_ref[0, 0] = _dot(mn, wkv_ref[0]).astype(BF16)


def _memory_kv(mem, norm_mem, wkv):
    b, m, d = mem.shape
    nl = wkv.shape[0]
    return pl.pallas_call(
        _kv_kernel,
        grid=(nl, b),
        in_specs=[
            pl.BlockSpec((1, m, d), lambda l, bi: (bi, 0, 0)),
            pl.BlockSpec((1, 1, d), lambda l, bi: (l, 0, 0)),
            pl.BlockSpec((1, d, 2 * d), lambda l, bi: (l, 0, 0)),
        ],
        out_specs=pl.BlockSpec((1, 1, m, 2 * d), lambda l, bi: (l, bi, 0, 0)),
        out_shape=jax.ShapeDtypeStruct((nl, b, m, 2 * d), BF16),
        compiler_params=_cparams("parallel", "parallel"),
        name="memory_kv",
    )(mem, norm_mem.reshape(nl, 1, d), wkv.astype(BF16))


def _xattn_kernel(h_ref, g_ref, wq_ref, kv_ref, wo_ref, o_ref):
    d = h_ref.shape[2]
    hd = d // XATTN_HEADS
    hb = h_ref[0]
    hn = _rms(hb, g_ref[...]).astype(BF16)
    q = (_dot(hn, wq_ref[...]) * (hd ** -0.5)).astype(BF16)
    kv = kv_ref[0]
    outs = []
    for a in range(XATTN_HEADS):
        qh = q[:, a * hd:(a + 1) * hd]
        kh = kv[:, a * hd:(a + 1) * hd]
        vh = kv[:, d + a * hd:d + (a + 1) * hd]
        sc = lax.dot_general(qh, kh, (((1,), (1,)), ((), ())), preferred_element_type=F32)
        e = jnp.exp(sc - jnp.max(sc, axis=-1, keepdims=True))
        o = _dot(e.astype(BF16), vh) / jnp.sum(e, axis=-1, keepdims=True)
        outs.append(o.astype(BF16))
    o_ref[0] = hb + _dot(jnp.concatenate(outs, axis=1), wo_ref[...])


def _cross_attn(h, g, wq, kv, wo):
    b, s, d = h.shape
    ts = SEQ_TILE
    m = kv.shape[1]
    return pl.pallas_call(
        _xattn_kernel,
        grid=(b, s // ts),
        in_specs=[
            pl.BlockSpec((1, ts, d), lambda bi, i: (bi, i, 0)),
            _const_spec((1, d)),
            _const_spec((d, d)),
            pl.BlockSpec((1, m, 2 * d), lambda bi, i: (bi, 0, 0)),
            _const_spec((d, d)),
        ],
        out_specs=pl.BlockSpec((1, ts, d), lambda bi, i: (bi, i, 0)),
        out_shape=jax.ShapeDtypeStruct((b, s, d), F32),
        compiler_params=_cparams("parallel", "parallel"),
        name="cross_attn",
    )(h, g.reshape(1, d), wq.astype(BF16), kv, wo.astype(BF16))


def _swiglu_kernel(h_ref, g_ref, wgu_ref, wd_ref, cast_src_ref, o_ref, cast_dst_ref):
    _cast_block(cast_src_ref, cast_dst_ref)
    ff = wd_ref.shape[0]
    hb = h_ref[...]
    hn = _rms(hb, g_ref[...]).astype(BF16)
    acc = hb
    for c in range(ff // FFN_CHUNK):
        lo = c * FFN_CHUNK
        gate = _dot(hn, wgu_ref[:, lo:lo + FFN_CHUNK])
        up = _dot(hn, wgu_ref[:, ff + lo:ff + lo + FFN_CHUNK])
        act = (gate * jax.nn.sigmoid(gate) * up).astype(BF16)
        acc = acc + _dot(act, wd_ref[lo:lo + FFN_CHUNK, :])
    o_ref[...] = acc


def _dense_swiglu(h2, g, w_gu, w_down, rider):
    t, d = h2.shape
    ts = SEQ_TILE
    rider_in, rider_out = rider.spec(lambda i: i)
    return pl.pallas_call(
        _swiglu_kernel,
        grid=(t // ts,),
        in_specs=[
            pl.BlockSpec((ts, d), lambda i: (i, 0)),
            _const_spec((1, d)),
            _const_spec(w_gu.shape),
            _const_spec(w_down.shape),
            rider_in,
        ],
        out_specs=[pl.BlockSpec((ts, d), lambda i: (i, 0)), rider_out],
        out_shape=[jax.ShapeDtypeStruct((t, d), F32), rider.out_shape],
        compiler_params=_cparams("parallel"),
        name="dense_swiglu",
    )(h2, g.reshape(1, d), w_gu.astype(BF16), w_down.astype(BF16), rider.w)


def _store_row_tiles(ref, x):
    for s in range(SUBLANES):
        ref[pl.ds(s, x.shape[0], stride=SUBLANES), :] = x[:, s * LANES:(s + 1) * LANES]


def _load_row_tiles(ref):
    rows = ref.shape[0] // SUBLANES
    return jnp.concatenate([ref[pl.ds(s, rows, stride=SUBLANES), :] for s in range(SUBLANES)], axis=1)


def _row_tile(ref, r):
    return ref.at[pl.ds(pl.multiple_of(r * SUBLANES, SUBLANES), SUBLANES), :]


def _router_kernel(h_ref, g_ref, wr_ref, hn_ref, meta_ref, total_ref, carry_ref):
    i = pl.program_id(0)
    tc = h_ref.shape[0]

    @pl.when(i == 0)
    def _():
        carry_ref[...] = jnp.zeros_like(carry_ref)

    @pl.when(i == pl.num_programs(0) - 1)
    def _():
        hn_ref[...] = jnp.zeros_like(hn_ref)

    @pl.when(i < pl.num_programs(0) - 1)
    def _():
        _route_chunk(h_ref, g_ref, wr_ref, hn_ref, meta_ref, total_ref, carry_ref)


def _route_chunk(h_ref, g_ref, wr_ref, hn_ref, meta_ref, total_ref, carry_ref):
    tc = h_ref.shape[0]
    hn = _rms(h_ref[...], g_ref[...])
    _store_row_tiles(hn_ref, hn)
    hn_hi = hn.astype(BF16)
    hn_lo = (hn - hn_hi.astype(F32)).astype(BF16)
    logits = (_dot(hn_hi, wr_ref[0]) + _dot(hn_lo, wr_ref[0]) + _dot(hn_hi, wr_ref[1]))
    lane = lax.broadcasted_iota(jnp.int32, (tc, LANES), 1)
    neg = jnp.float32(-jnp.inf)
    l1 = jnp.where(lane < N_EXPERTS, logits, neg)
    m1 = jnp.max(l1, axis=-1, keepdims=True)
    i1 = jnp.min(jnp.where(l1 == m1, lane, LANES), axis=-1, keepdims=True)
    l2 = jnp.where(lane == i1, neg, l1)
    m2 = jnp.max(l2, axis=-1, keepdims=True)
    i2 = jnp.min(jnp.where(l2 == m2, lane, LANES), axis=-1, keepdims=True)
    ex = jnp.exp(m2 - m1)
    w1 = 1.0 / (1.0 + ex)
    w2 = ex / (1.0 + ex)
    sel1 = lane == i1
    sel2 = lane == i2
    onehot = (sel1 | sel2).astype(BF16)
    r = lax.broadcasted_iota(jnp.int32, (tc, tc), 0)
    c = lax.broadcasted_iota(jnp.int32, (tc, tc), 1)
    lower = (c < r).astype(BF16)
    carry = carry_ref[...]
    prior = _dot(lower, onehot) + carry
    rank1 = jnp.sum(jnp.where(sel1, prior, 0.0), axis=-1, keepdims=True)
    rank2 = jnp.sum(jnp.where(sel2, prior, 0.0), axis=-1, keepdims=True)
    meta = jnp.where(lane == 0, i1.astype(F32), 0.0)
    meta = jnp.where(lane == 1, i2.astype(F32), meta)
    meta = jnp.where(lane == 2, rank1, meta)
    meta = jnp.where(lane == 3, rank2, meta)
    meta = jnp.where(lane == 4, w1, meta)
    meta = jnp.where(lane == 5, w2, meta)
    meta_ref[...] = meta
    new_carry = carry + jnp.sum(onehot.astype(F32), axis=0, keepdims=True)
    carry_ref[...] = new_carry
    total_ref[...] = new_carry


def _router(h2, g, router_w):
    t, d = h2.shape
    tc = TOK_CHUNK
    nc = t // tc
    wr = jnp.zeros((d, LANES), F32).at[:, :router_w.shape[1]].set(router_w)
    wr_hi = wr.astype(BF16)
    wr = jnp.stack([wr_hi, (wr - wr_hi.astype(F32)).astype(BF16)])
    last = nc - 1
    return pl.pallas_call(
        _router_kernel,
        grid=(nc + 1,),
        in_specs=[
            pl.BlockSpec((tc, d), lambda i: (jnp.minimum(i, last), 0)),
            _const_spec((1, d)),
            _const_spec((2, d, LANES)),
        ],
        out_specs=[
            pl.BlockSpec((tc * SUBLANES, LANES), lambda i: (i, 0)),
            pl.BlockSpec((tc, LANES), lambda i: (jnp.minimum(i, last), 0)),
            pl.BlockSpec((1, LANES), lambda i: (0, 0)),
        ],
        out_shape=[
            jax.ShapeDtypeStruct(((t + tc) * SUBLANES, LANES), F32),
            jax.ShapeDtypeStruct((t, LANES), F32),
            jax.ShapeDtypeStruct((1, LANES), F32),
        ],
        scratch_shapes=[pltpu.VMEM((1, LANES), F32)],
        compiler_params=_cparams("arbitrary"),
        name="moe_router",
    )(h2, g.reshape(1, d), wr)


def _row_copy(src, dst, sem):
    return pltpu.make_async_copy(src, dst, sem)


def _experts_kernel(te_ref, nt_ref, nreal_ref, inv_cur_ref, inv_nxt_ref, inv_nx2_ref, inv_prv_ref,
                    hn_hbm, wg_ref, wu_ref, wd_ref, y_hbm, xin_ref, x_ref, acc_ref, yout_ref,
                    gsem, ssem, *, n_tokens):
    q = pl.program_id(0)
    f = pl.program_id(1)
    nf = pl.num_programs(1)
    nt = nt_ref[0]
    rows = x_ref.shape[0]
    live = q < nt
    slot = q % 2
    other = 1 - slot
    xslot = q % 3
    xahead = (q + 2) % 3
    share = -(-rows // MOE_STEPS)
    last_tile = nreal_ref.shape[0] - 1
    n_prev = jnp.where(q >= 1, nreal_ref[jnp.clip(q - 1, 0, last_tile)], 0)
    n_prev2 = nreal_ref[jnp.clip(q - 2, 0, last_tile)]

    def token_row(a):
        return jnp.where(a < 0, n_tokens, jnp.where(a >= n_tokens, a - n_tokens, a))

    def gather_row(src, dst_slot, j):
        _row_copy(_row_tile(hn_hbm, src), _row_tile(xin_ref.at[dst_slot], j), gsem.at[dst_slot]).start()

    def scatter_row(src_slot, j, dst):
        _row_copy(_row_tile(yout_ref.at[src_slot], j), _row_tile(y_hbm, dst), ssem.at[src_slot]).start()

    def wait_gathered(s):
        _row_copy(hn_hbm.at[pl.ds(0, rows * SUBLANES), :], xin_ref.at[s], gsem.at[s]).wait()

    def wait_scattered(s, n):
        size = pl.multiple_of(n * SUBLANES, SUBLANES)
        _row_copy(yout_ref.at[s, pl.ds(0, size), :], y_hbm.at[pl.ds(0, size), :], ssem.at[s]).wait()

    @pl.when((q == 0) & (f == 0))
    def _():
        def one(j, c):
            gather_row(token_row(inv_cur_ref[0, 0, j]), 0, j)
            gather_row(token_row(inv_nxt_ref[0, 0, j]), 1, j)
            return c
        lax.fori_loop(0, rows, one, 0, unroll=ROW_BATCH)

    @pl.when((q >= 2) & (q <= nt) & (f == 0))
    def _():
        wait_scattered(slot, n_prev2)

    @pl.when((q <= nt + 1) & (f == 0))
    def _():
        wait_gathered(xslot)

    @pl.when(live & (f == 0))
    def _():
        x_ref[...] = _load_row_tiles(xin_ref.at[xslot]).astype(BF16)
        acc_ref[...] = jnp.zeros_like(acc_ref)

    @pl.when(live)
    def _():
        lo = f * share

        def issue_batch(b):
            base = lo + b * ROW_BATCH
            srcs = [token_row(inv_nx2_ref[0, 0, base + u]) for u in range(ROW_BATCH)]
            dsts = [inv_prv_ref[0, 0, base + u] for u in range(ROW_BATCH)]
            for u in range(ROW_BATCH):
                gather_row(srcs[u], xahead, base + u)
            for u in range(ROW_BATCH):
                @pl.when(base + u < n_prev)
                def _(u=u):
                    scatter_row(other, base + u, dsts[u])

        n_sub = wd_ref.shape[1] // FFN_CHUNK
        n_batches = share // ROW_BATCH
        n_issue = max(1, n_sub - 3)
        x = x_ref[...]
        acc = acc_ref[...]
        for c in range(n_sub):
            if c < n_issue:
                for b in range(c * n_batches // n_issue, (c + 1) * n_batches // n_issue):
                    issue_batch(b)
            cols = slice(c * FFN_CHUNK, (c + 1) * FFN_CHUNK)
            gate = _dot(x, wg_ref[0, :, cols])
            up = _dot(x, wu_ref[0, :, cols])
            act = (gate * jax.nn.sigmoid(gate) * up).astype(BF16)
            acc = acc + _dot(act, wd_ref[0, cols, :])
        acc_ref[...] = acc

    @pl.when(live & (f == nf - 1))
    def _():
        _store_row_tiles(yout_ref.at[slot], acc_ref[...])

    @pl.when((q == nt) & (f == 0))
    def _():
        def one(j, c):
            scatter_row(other, j, inv_prv_ref[0, 0, j])
            return c
        lax.fori_loop(0, n_prev, one, 0)
        wait_scattered(other, n_prev)


def _experts(hn_rows, inv, tile_expert, n_tiles, tile_real, w_gate, w_up, w_down, n_tokens):
    d = SUBLANES * LANES
    ne, ff, _ = w_down.shape
    assert w_down.shape[2] == d
    nf = ff // MOE_CHUNK
    assert nf == MOE_STEPS and MOE_ROWS % (MOE_STEPS * ROW_BATCH) == 0
    nq = inv.shape[0]

    def tile(q, nt):
        return jnp.minimum(q, nt[0] - 1)

    def chunk(q, f, nt):
        return jnp.where(q < nt[0], f, nf - 1)

    idx_block = (1, 1, MOE_ROWS)
    return pl.pallas_call(
        functools.partial(_experts_kernel, n_tokens=n_tokens),
        grid_spec=pltpu.PrefetchScalarGridSpec(
            num_scalar_prefetch=3,
            grid=(nq + 2, nf),
            in_specs=[
                pl.BlockSpec(idx_block, lambda q, f, te, nt, nr: (jnp.minimum(q, nq - 1), 0, 0),
                             memory_space=pltpu.SMEM),
                pl.BlockSpec(idx_block, lambda q, f, te, nt, nr: (jnp.minimum(q + 1, nq - 1), 0, 0),
                             memory_space=pltpu.SMEM),
                pl.BlockSpec(idx_block, lambda q, f, te, nt, nr: (jnp.minimum(q + 2, nq - 1), 0, 0),
                             memory_space=pltpu.SMEM),
                pl.BlockSpec(idx_block, lambda q, f, te, nt, nr: (jnp.clip(q - 1, 0, nq - 1), 0, 0),
                             memory_space=pltpu.SMEM),
                pl.BlockSpec(memory_space=pl.ANY),
                pl.BlockSpec((1, d, MOE_CHUNK),
                             lambda q, f, te, nt, nr: (te[tile(q, nt)], 0, chunk(q, f, nt))),
                pl.BlockSpec((1, d, MOE_CHUNK),
                             lambda q, f, te, nt, nr: (te[tile(q, nt)], 0, chunk(q, f, nt))),
                pl.BlockSpec((1, MOE_CHUNK, d),
                             lambda q, f, te, nt, nr: (te[tile(q, nt)], chunk(q, f, nt), 0)),
            ],
            out_specs=pl.BlockSpec(memory_space=pl.ANY),
            scratch_shapes=[
                pltpu.VMEM((3, MOE_ROWS * SUBLANES, LANES), F32),
                pltpu.VMEM((MOE_ROWS, d), BF16),
                pltpu.VMEM((MOE_ROWS, d), F32),
                pltpu.VMEM((2, MOE_ROWS * SUBLANES, LANES), F32),
                pltpu.SemaphoreType.DMA((3,)),
                pltpu.SemaphoreType.DMA((2,)),
            ],
        ),
        out_shape=jax.ShapeDtypeStruct((2 * n_tokens * SUBLANES, LANES), F32),
        compiler_params=pltpu.CompilerParams(
            dimension_semantics=("arbitrary", "arbitrary"), vmem_limit_bytes=VMEM_LIMIT,
            disable_bounds_checks=True),
        name="moe_experts",
    )(tile_expert, n_tiles, tile_real, inv, inv, inv, inv, hn_rows, w_gate, w_up, w_down)


def _invert_kernel(dest_ref, unset_hbm, inv_ref, sem):
    fill = pltpu.make_async_copy(unset_hbm, inv_ref, sem)
    fill.start()
    fill.wait()
    batch = 2 * ROW_BATCH

    def put(b, c):
        a0 = b * batch
        rows = [dest_ref[a0 + u] for u in range(batch)]
        for u in range(batch):
            inv_ref[rows[u]] = a0 + u
        return c

    lax.fori_loop(0, dest_ref.shape[0] // batch, put, 0)


def _invert(dest, n_rows):
    assert dest.shape[0] % (2 * ROW_BATCH) == 0
    return pl.pallas_call(
        _invert_kernel,
        in_specs=[pl.BlockSpec(memory_space=pltpu.SMEM), pl.BlockSpec(memory_space=pl.ANY)],
        out_specs=pl.BlockSpec(memory_space=pltpu.SMEM),
        out_shape=jax.ShapeDtypeStruct((n_rows,), jnp.int32),
        scratch_shapes=[pltpu.SemaphoreType.DMA(())],
        name="moe_invert",
    )(dest, jnp.full((n_rows,), -1, jnp.int32))


def _combine_kernel(h_ref, y1_ref, y2_ref, wts_ref, g_ref, o_ref):
    wt = wts_ref[...]
    moe = wt[:, 0:1] * _load_row_tiles(y1_ref) + wt[:, 1:2] * _load_row_tiles(y2_ref)
    o_ref[...] = _rms(h_ref[...] + moe, g_ref[...])


def _combine(h2, y, wts, g_final):
    t, d = h2.shape
    ts = SEQ_TILE
    nt = t // ts
    return pl.pallas_call(
        _combine_kernel,
        grid=(nt,),
        in_specs=[
            pl.BlockSpec((ts, d), lambda i: (i, 0)),
            pl.BlockSpec((ts * SUBLANES, LANES), lambda i: (i, 0)),
            pl.BlockSpec((ts * SUBLANES, LANES), lambda i: (nt + i, 0)),
            pl.BlockSpec((ts, 2), lambda i: (i, 0)),
            _const_spec((1, d)),
        ],
        out_specs=pl.BlockSpec((ts, d), lambda i: (i, 0)),
        out_shape=jax.ShapeDtypeStruct((t, d), F32),
        compiler_params=_cparams("parallel"),
        name="moe_combine",
    )(h2, y, y, wts, g_final.reshape(1, d))


def _moe_block(h2, g, router_w, w_gate, w_up, w_down, g_final):
    t, d = h2.shape
    ne = w_down.shape[0]
    nq = (2 * t) // MOE_ROWS + ne
    n_rows = nq * MOE_ROWS

    hn_rows, meta, total = _router(h2, g, router_w)

    experts = meta[:, 0:2].astype(jnp.int32)
    ranks = meta[:, 2:4].astype(jnp.int32)
    wts = meta[:, 4:6]
    counts = total[0, :ne].astype(jnp.int32)
    tiles_per = (counts + MOE_ROWS - 1) // MOE_ROWS
    tile_end = jnp.cumsum(tiles_per)
    tile_start = tile_end - tiles_per
    n_tiles = tile_end[-1]
    dest = (jnp.take(tile_start * MOE_ROWS, experts) + ranks).T.reshape(-1)
    qs = jnp.arange(nq, dtype=jnp.int32)
    tile_expert = jnp.minimum(jnp.sum((qs[:, None] >= tile_end[None, :]).astype(jnp.int32), axis=1),
                              ne - 1)
    tile_real = jnp.clip(jnp.take(counts, tile_expert) - (qs - jnp.take(tile_start, tile_expert)) * MOE_ROWS,
                         0, MOE_ROWS)
    inv = _invert(dest, n_rows)
    y = _experts(hn_rows, inv.reshape(nq, 1, MOE_ROWS), tile_expert, n_tiles.reshape(1), tile_real,
                 w_gate, w_up, w_down, t)
    return _combine(h2, y, wts, g_final)


def kernel(x, mem, ev_norm_mix, ev_w_in, ev_conv_a, ev_pool_w, ev_pool_scale, ev_w_out, ev_norm_ffn, ev_ffn_gu, ev_ffn_down, od_norm_mix, od_pw1_w, od_pw1_b, od_dw_w, od_dw_b, od_ln_g, od_ln_b, od_pw2_w, od_pw2_b, od_norm_moe, od_router, od_moe_gu, od_moe_down, xa_norm, xa_norm_mem, xa_wq, xa_wkv, xa_wo, final_norm):
    b, s, d = x.shape
    assert ev_w_in.shape[0] == 1 and od_pw1_w.shape[0] == 1 and xa_wq.shape[0] == 2
    kv = _memory_kv(mem, xa_norm_mem, xa_wkv)
    n_steps = (b * s) // SEQ_TILE
    h, (moe_down, ffn_gu, ffn_down, wq, wo, pw1_w, pw2_w) = _even_mixer(
        x, ev_norm_mix[0], ev_w_in[0], ev_conv_a[0], ev_pool_w[0], ev_pool_scale[0], ev_w_out[0],
        [_CastRider(od_moe_down[0], n_steps), _CastRider(ev_ffn_gu[0], n_steps),
         _CastRider(ev_ffn_down[0], n_steps, repeat=2), _CastRider(xa_wq, n_steps),
         _CastRider(xa_wo, n_steps), _CastRider(od_pw1_w[0], n_steps),
         _CastRider(od_pw2_w[0], n_steps)])
    h = _cross_attn(h, xa_norm[0], wq[0], kv[0], wo[0])
    h, moe_gate = _dense_swiglu(h.reshape(b * s, d), ev_norm_ffn[0], ffn_gu[0], ffn_down[0],
                                _CastRider(od_moe_gu[0], n_steps, col_blocks=2, col_block=0))
    h, moe_up = _conformer(h.reshape(b, s, d), od_norm_mix[0], pw1_w[0], od_pw1_b[0], od_dw_w[0],
                           od_dw_b[0], od_ln_g[0], od_ln_b[0], pw2_w[0], od_pw2_b[0],
                           _CastRider(od_moe_gu[0], n_steps, col_blocks=2, col_block=1))
    h = _cross_attn(h, xa_norm[1], wq[1], kv[1], wo[1])
    out = _moe_block(h.reshape(b * s, d), od_norm_moe[0], od_router[0], moe_gate, moe_up,
                     moe_down, final_norm)
    return out.reshape(b, s, d)
```

```python
import functools

import jax
import jax.numpy as jnp
from jax import lax
from jax.experimental import pallas as pl
from jax.experimental.pallas import tpu as pltpu

EPS = 1e-6
BF16 = jnp.bfloat16
F32 = jnp.float32

POOL_WINDOWS = (2, 4, 8, 16)
XATTN_HEADS = 4
N_EXPERTS = 8
LANES = 128
SUBLANES = 8
BF16_TILE_ROWS = 16
VMEM_LIMIT = 56 * 1024 * 1024

SEQ_TILE = 512
MIX_HALO = 8
CONF_HALO = 16
CONV_ROWS = 32
GLU_PITCH = 2
FFN_CHUNK = 256
MOE_ROWS = 512
MOE_CHUNK = 1792
MOE_STEPS = 2
ROW_BATCH = 8
TOK_CHUNK = 512


def _rms(x, g):
    return x * lax.rsqrt(jnp.mean(x * x, axis=-1, keepdims=True) + EPS) * g


def _dot(a, b):
    return jnp.dot(a, b, preferred_element_type=F32)


def _cparams(*sem):
    return pltpu.CompilerParams(dimension_semantics=sem, vmem_limit_bytes=VMEM_LIMIT)


def _const_spec(shape):
    nd = len(shape)
    return pl.BlockSpec(shape, lambda *_: (0,) * nd, pipeline_mode=pl.Buffered(1))


class _CastRider:
    def __init__(self, w, n_steps, col_blocks=1, col_block=0, repeat=1):
        if w.ndim == 2:
            w = w[None]
        e, r, c = w.shape
        per_expert = n_steps // (e * repeat)
        self.w = w
        self.block = (1, r // per_expert, c // col_blocks)
        assert self.block[1] % BF16_TILE_ROWS == 0 and r % per_expert == 0
        self.out_shape = jax.ShapeDtypeStruct((e, r, c // col_blocks), BF16)
        self._per_expert = per_expert
        self._col_block = col_block
        self._repeat = repeat

    def spec(self, step_of):
        pe, cb, rep = self._per_expert, self._col_block, self._repeat

        def in_map(*idx):
            s = step_of(*idx) // rep
            return (s // pe, s % pe, cb)

        def out_map(*idx):
            s = step_of(*idx) // rep
            return (s // pe, s % pe, 0)

        return pl.BlockSpec(self.block, in_map), pl.BlockSpec(self.block, out_map)


def _cast_block(src_ref, dst_ref):
    dst_ref[...] = src_ref[...].astype(BF16)


def _even_mixer_kernel(x_ref, xp_ref, xn_ref, g_ref, win_ref, conv_ref, pw_ref,
                       ps_ref, wout_ref, *rest, seq_len, n_riders):
    cast_src_refs, o_ref = rest[:n_riders], rest[n_riders]
    cast_dst_refs = rest[n_riders + 1:2 * n_riders + 1]
    z_ref, sh_ref = rest[2 * n_riders + 1:]
    for src_ref, dst_ref in zip(cast_src_refs, cast_dst_refs):
        _cast_block(src_ref, dst_ref)
    i = pl.program_id(1)
    n_i = pl.num_programs(1)
    ts = x_ref.shape[1]
    h0 = MIX_HALO
    aw = conv_ref.shape[1]
    bg = pw_ref.shape[1]
    xm = x_ref[0]
    xa = jnp.concatenate([xp_ref[0], xm, xn_ref[0]], axis=0)
    hn = _rms(xa, g_ref[...]).astype(BF16)
    z = _dot(hn, win_ref[...])
    row = lax.broadcasted_iota(jnp.int32, (ts + 2 * h0, 1), 0)
    inside = ((row >= h0) | (i > 0)) & ((row < ts + h0) | (i < n_i - 1))
    z_ref[...] = jnp.where(inside, z, 0.0)

    n_a = aw // LANES
    n_rows = ts + 2 * h0

    def shifted(slab, shift):
        return sh_ref[slab, pl.ds((h0 + shift) * GLU_PITCH, ts, stride=GLU_PITCH), :]

    for c in range(n_a):
        lo = c * LANES
        sh_ref[c, pl.ds(0, n_rows, stride=GLU_PITCH), :] = (z_ref[:, 2 * aw + lo:2 * aw + lo + LANES]
                                                           * z_ref[:, lo:lo + LANES])
    for gi in range(len(POOL_WINDOWS)):
        c0 = 3 * aw + gi * bg
        sh_ref[n_a + gi, pl.ds(0, n_rows, stride=GLU_PITCH), :] = z_ref[:, c0:c0 + bg]

    cw = conv_ref[...]
    conv = jnp.concatenate(
        [cw[0:1, c * LANES:(c + 1) * LANES] * shifted(c, -1)
         + cw[1:2, c * LANES:(c + 1) * LANES] * shifted(c, 0)
         + cw[2:3, c * LANES:(c + 1) * LANES] * shifted(c, 1) for c in range(n_a)], axis=1)
    parts = [(z_ref[h0:h0 + ts, aw:2 * aw] * conv).astype(BF16)]

    pos = i * ts + lax.broadcasted_iota(jnp.int32, (ts, 1), 0)
    for gi, win in enumerate(POOL_WINDOWS):
        left = win // 2
        right = win - 1 - left
        c0 = 3 * aw + gi * bg
        s = shifted(n_a + gi, -left)
        for j in range(-left + 1, right + 1):
            s = s + shifted(n_a + gi, j)
        cnt = (jnp.minimum(pos + right, seq_len - 1) - jnp.maximum(pos - left, 0) + 1).astype(F32)
        p = s / cnt - z_ref[h0:h0 + ts, c0:c0 + bg]
        yb = _dot(p.astype(BF16), pw_ref[gi]) * ps_ref[:, gi * bg:(gi + 1) * bg]
        parts.append(yb.astype(BF16))
    y = jnp.concatenate(parts, axis=1)
    o_ref[0] = xm + _dot(y, wout_ref[...])


def _even_mixer(x, g, w_in, conv_a, pool_w, pool_scale, w_out, riders):
    b, s, d = x.shape
    ts, h0 = SEQ_TILE, MIX_HALO
    rider_specs = [rd.spec(lambda bi, i: bi * (s // ts) + i) for rd in riders]
    r = ts // h0
    incols = w_in.shape[1]
    aw = conv_a.shape[1]
    assert pool_w.shape[1] == LANES and aw % LANES == 0
    h, *casts = pl.pallas_call(
        functools.partial(_even_mixer_kernel, seq_len=s, n_riders=len(riders)),
        grid=(b, s // ts),
        in_specs=[
            pl.BlockSpec((1, ts, d), lambda bi, i: (bi, i, 0)),
            pl.BlockSpec((1, h0, d), lambda bi, i: (bi, jnp.maximum(i * r - 1, 0), 0)),
            pl.BlockSpec((1, h0, d), lambda bi, i: (bi, jnp.minimum((i + 1) * r, s // h0 - 1), 0)),
            _const_spec((1, d)),
            _const_spec((d, incols)),
            _const_spec(conv_a.shape),
            _const_spec(pool_w.shape),
            _const_spec((1, pool_scale.shape[-1])),
            _const_spec(w_out.shape),
        ] + [sp[0] for sp in rider_specs],
        out_specs=[pl.BlockSpec((1, ts, d), lambda bi, i: (bi, i, 0))] + [sp[1] for sp in rider_specs],
        out_shape=[jax.ShapeDtypeStruct((b, s, d), F32)] + [rd.out_shape for rd in riders],
        scratch_shapes=[pltpu.VMEM((ts + 2 * h0, incols), F32),
                        pltpu.VMEM(((incols - 2 * aw) // LANES, GLU_PITCH * (ts + 2 * h0), LANES), F32)],
        compiler_params=_cparams("arbitrary", "arbitrary"),
        name="even_mixer",
    )(x, x, x, g.reshape(1, d), w_in.astype(BF16), conv_a, pool_w.astype(BF16),
      pool_scale.reshape(1, -1), w_out.astype(BF16), *[rd.w for rd in riders])
    return h, casts


def _conformer_kernel(x_ref, xp_ref, xn_ref, g_ref, pw1_ref, b1_ref, dw_ref, dwb_ref,
                      lng_ref, lnb_ref, pw2_ref, b2_ref, cast_src_ref, o_ref, cast_dst_ref,
                      glu_ref, conv_ref):
    _cast_block(cast_src_ref, cast_dst_ref)
    i = pl.program_id(1)
    n_i = pl.num_programs(1)
    ts = x_ref.shape[1]
    d = x_ref.shape[2]
    h0 = CONF_HALO
    taps = dw_ref.shape[0] // SUBLANES
    half = taps // 2
    xm = x_ref[0]
    xa = jnp.concatenate([xp_ref[0], xm, xn_ref[0]], axis=0)
    hn = _rms(xa, g_ref[...]).astype(BF16)
    y1 = _dot(hn, pw1_ref[...]) + b1_ref[...]
    glu = y1[:, :d] * jax.nn.sigmoid(y1[:, d:])
    row = lax.broadcasted_iota(jnp.int32, (ts + 2 * h0, 1), 0)
    inside = ((row >= h0) | (i > 0)) & ((row < ts + h0) | (i < n_i - 1))
    glu = jnp.where(inside, glu, 0.0)
    n_rows = ts + 2 * h0
    for c in range(d // LANES):
        glu_ref[c, pl.ds(0, n_rows, stride=GLU_PITCH), :] = glu[:, c * LANES:(c + 1) * LANES]

    n_sub = CONV_ROWS // SUBLANES
    for c in range(d // LANES):
        cols = slice(c * LANES, (c + 1) * LANES)
        w_taps = [dw_ref[SUBLANES * k:SUBLANES * (k + 1), cols] for k in range(taps)]
        bias = dwb_ref[:, cols]

        def conv_block(rb, carry, c=c, cols=cols, w_taps=w_taps, bias=bias):
            base = pl.multiple_of(rb * CONV_ROWS, CONV_ROWS)
            accs = [jnp.zeros((SUBLANES, LANES), F32) + bias for _ in range(n_sub)]
            for k in range(taps):
                first = base + (h0 - half + k)
                for a in range(n_sub):
                    rows8 = pl.ds((first + a * SUBLANES) * GLU_PITCH, SUBLANES, stride=GLU_PITCH)
                    accs[a] = accs[a] + w_taps[k] * glu_ref[c, rows8, :]
            conv_ref[pl.ds(base, CONV_ROWS), cols] = jnp.concatenate(accs, axis=0)
            return carry

        lax.fori_loop(0, ts // CONV_ROWS, conv_block, 0)

    acc = conv_ref[...]
    mu = jnp.mean(acc, axis=-1, keepdims=True)
    xc = acc - mu
    var = jnp.mean(xc * xc, axis=-1, keepdims=True)
    yn = xc * lax.rsqrt(var + EPS) * lng_ref[...] + lnb_ref[...]
    act = (yn * jax.nn.sigmoid(yn)).astype(BF16)
    o_ref[0] = xm + _dot(act, pw2_ref[...]) + b2_ref[...]


def _conformer(x, g, pw1_w, pw1_b, dw_w, dw_b, ln_g, ln_b, pw2_w, pw2_b, rider):
    b, s, d = x.shape
    ts, h0 = SEQ_TILE, CONF_HALO
    rider_in, rider_out = rider.spec(lambda bi, i: bi * (s // ts) + i)
    r = ts // h0
    return pl.pallas_call(
        _conformer_kernel,
        grid=(b, s // ts),
        in_specs=[
            pl.BlockSpec((1, ts, d), lambda bi, i: (bi, i, 0)),
            pl.BlockSpec((1, h0, d), lambda bi, i: (bi, jnp.maximum(i * r - 1, 0), 0)),
            pl.BlockSpec((1, h0, d), lambda bi, i: (bi, jnp.minimum((i + 1) * r, s // h0 - 1), 0)),
            _const_spec((1, d)),
            _const_spec(pw1_w.shape),
            _const_spec((1, 2 * d)),
            _const_spec((dw_w.shape[0] * SUBLANES, d)),
            _const_spec((1, d)),
            _const_spec((1, d)),
            _const_spec((1, d)),
            _const_spec(pw2_w.shape),
            _const_spec((1, d)),
            rider_in,
        ],
        out_specs=[pl.BlockSpec((1, ts, d), lambda bi, i: (bi, i, 0)), rider_out],
        out_shape=[jax.ShapeDtypeStruct((b, s, d), F32), rider.out_shape],
        scratch_shapes=[pltpu.VMEM((d // LANES, GLU_PITCH * (ts + 2 * h0), LANES), F32),
                        pltpu.VMEM((ts, d), F32)],
        compiler_params=_cparams("parallel", "parallel"),
        name="conformer",
    )(x, x, x, g.reshape(1, d), pw1_w.astype(BF16), pw1_b.reshape(1, -1),
      jnp.repeat(dw_w, SUBLANES, axis=0),
      dw_b.reshape(1, d), ln_g.reshape(1, d), ln_b.reshape(1, d), pw2_w.astype(BF16),
      pw2_b.reshape(1, d), rider.w)


def _kv_kernel(mem_ref, g_ref, wkv_ref, kv_ref):
    mn = _rms(mem_ref[0], g_ref[0]).astype(BF16)
    kv_ref[0, 0] = _dot(mn, wkv_ref[0]).astype(BF16)


def _memory_kv(mem, norm_mem, wkv):
    b, m, d = mem.shape
    nl = wkv.shape[0]
    return pl.pallas_call(
        _kv_kernel,
        grid=(nl, b),
        in_specs=[
            pl.BlockSpec((1, m, d), lambda l, bi: (bi, 0, 0)),
            pl.BlockSpec((1, 1, d), lambda l, bi: (l, 0, 0)),
            pl.BlockSpec((1, d, 2 * d), lambda l, bi: (l, 0, 0)),
        ],
        out_specs=pl.BlockSpec((1, 1, m, 2 * d), lambda l, bi: (l, bi, 0, 0)),
        out_shape=jax.ShapeDtypeStruct((nl, b, m, 2 * d), BF16),
        compiler_params=_cparams("parallel", "parallel"),
        name="memory_kv",
    )(mem, norm_mem.reshape(nl, 1, d), wkv.astype(BF16))


def _xattn_kernel(h_ref, g_ref, wq_ref, kv_ref, wo_ref, o_ref):
    d = h_ref.shape[2]
    hd = d // XATTN_HEADS
    hb = h_ref[0]
    hn = _rms(hb, g_ref[...]).astype(BF16)
    q = (_dot(hn, wq_ref[...]) * (hd ** -0.5)).astype(BF16)
    kv = kv_ref[0]
    outs = []
    for a in range(XATTN_HEADS):
        qh = q[:, a * hd:(a + 1) * hd]
        kh = kv[:, a * hd:(a + 1) * hd]
        vh = kv[:, d + a * hd:d + (a + 1) * hd]
        sc = lax.dot_general(qh, kh, (((1,), (1,)), ((), ())), preferred_element_type=F32)
        e = jnp.exp(sc - jnp.max(sc, axis=-1, keepdims=True))
        o = _dot(e.astype(BF16), vh) / jnp.sum(e, axis=-1, keepdims=True)
        outs.append(o.astype(BF16))
    o_ref[0] = hb + _dot(jnp.concatenate(outs, axis=1), wo_ref[...])


def _cross_attn(h, g, wq, kv, wo):
    b, s, d = h.shape
    ts = SEQ_TILE
    m = kv.shape[1]
    return pl.pallas_call(
        _xattn_kernel,
        grid=(b, s // ts),
        in_specs=[
            pl.BlockSpec((1, ts, d), lambda bi, i: (bi, i, 0)),
            _const_spec((1, d)),
            _const_spec((d, d)),
            pl.BlockSpec((1, m, 2 * d), lambda bi, i: (bi, 0, 0)),
            _const_spec((d, d)),
        ],
        out_specs=pl.BlockSpec((1, ts, d), lambda bi, i: (bi, i, 0)),
        out_shape=jax.ShapeDtypeStruct((b, s, d), F32),
        compiler_params=_cparams("parallel", "parallel"),
        name="cross_attn",
    )(h, g.reshape(1, d), wq.astype(BF16), kv, wo.astype(BF16))


def _swiglu_kernel(h_ref, g_ref, wgu_ref, wd_ref, cast_src_ref, o_ref, cast_dst_ref):
    _cast_block(cast_src_ref, cast_dst_ref)
    ff = wd_ref.shape[0]
    hb = h_ref[...]
    hn = _rms(hb, g_ref[...]).astype(BF16)
    acc = hb
    for c in range(ff // FFN_CHUNK):
        lo = c * FFN_CHUNK
        gate = _dot(hn, wgu_ref[:, lo:lo + FFN_CHUNK])
        up = _dot(hn, wgu_ref[:, ff + lo:ff + lo + FFN_CHUNK])
        act = (gate * jax.nn.sigmoid(gate) * up).astype(BF16)
        acc = acc + _dot(act, wd_ref[lo:lo + FFN_CHUNK, :])
    o_ref[...] = acc


def _dense_swiglu(h2, g, w_gu, w_down, rider):
    t, d = h2.shape
    ts = SEQ_TILE
    rider_in, rider_out = rider.spec(lambda i: i)
    return pl.pallas_call(
        _swiglu_kernel,
        grid=(t // ts,),
        in_specs=[
            pl.BlockSpec((ts, d), lambda i: (i, 0)),
            _const_spec((1, d)),
            _const_spec(w_gu.shape),
            _const_spec(w_down.shape),
            rider_in,
        ],
        out_specs=[pl.BlockSpec((ts, d), lambda i: (i, 0)), rider_out],
        out_shape=[jax.ShapeDtypeStruct((t, d), F32), rider.out_shape],
        compiler_params=_cparams("parallel"),
        name="dense_swiglu",
    )(h2, g.reshape(1, d), w_gu.astype(BF16), w_down.astype(BF16), rider.w)


def _store_row_tiles(ref, x):
    for s in range(SUBLANES):
        ref[pl.ds(s, x.shape[0], stride=SUBLANES), :] = x[:, s * LANES:(s + 1) * LANES]


def _load_row_tiles(ref):
    rows = ref.shape[0] // SUBLANES
    return jnp.concatenate([ref[pl.ds(s, rows, stride=SUBLANES), :] for s in range(SUBLANES)], axis=1)


def _row_tile(ref, r):
    return ref.at[pl.ds(pl.multiple_of(r * SUBLANES, SUBLANES), SUBLANES), :]


def _router_kernel(h_ref, g_ref, wr_ref, hn_ref, meta_ref, total_ref, carry_ref):
    i = pl.program_id(0)
    tc = h_ref.shape[0]

    @pl.when(i == 0)
    def _():
        carry_ref[...] = jnp.zeros_like(carry_ref)

    @pl.when(i == pl.num_programs(0) - 1)
    def _():
        hn_ref[...] = jnp.zeros_like(hn_ref)

    @pl.when(i < pl.num_programs(0) - 1)
    def _():
        _route_chunk(h_ref, g_ref, wr_ref, hn_ref, meta_ref, total_ref, carry_ref)


def _route_chunk(h_ref, g_ref, wr_ref, hn_ref, meta_ref, total_ref, carry_ref):
    tc = h_ref.shape[0]
    hn = _rms(h_ref[...], g_ref[...])
    _store_row_tiles(hn_ref, hn)
    hn_hi = hn.astype(BF16)
    hn_lo = (hn - hn_hi.astype(F32)).astype(BF16)
    logits = (_dot(hn_hi, wr_ref[0]) + _dot(hn_lo, wr_ref[0]) + _dot(hn_hi, wr_ref[1]))
    lane = lax.broadcasted_iota(jnp.int32, (tc, LANES), 1)
    neg = jnp.float32(-jnp.inf)
    l1 = jnp.where(lane < N_EXPERTS, logits, neg)
    m1 = jnp.max(l1, axis=-1, keepdims=True)
    i1 = jnp.min(jnp.where(l1 == m1, lane, LANES), axis=-1, keepdims=True)
    l2 = jnp.where(lane == i1, neg, l1)
    m2 = jnp.max(l2, axis=-1, keepdims=True)
    i2 = jnp.min(jnp.where(l2 == m2, lane, LANES), axis=-1, keepdims=True)
    ex = jnp.exp(m2 - m1)
    w1 = 1.0 / (1.0 + ex)
    w2 = ex / (1.0 + ex)
    sel1 = lane == i1
    sel2 = lane == i2
    onehot = (sel1 | sel2).astype(BF16)
    r = lax.broadcasted_iota(jnp.int32, (tc, tc), 0)
    c = lax.broadcasted_iota(jnp.int32, (tc, tc), 1)
    lower = (c < r).astype(BF16)
    carry = carry_ref[...]
    prior = _dot(lower, onehot) + carry
    rank1 = jnp.sum(jnp.where(sel1, prior, 0.0), axis=-1, keepdims=True)
    rank2 = jnp.sum(jnp.where(sel2, prior, 0.0), axis=-1, keepdims=True)
    meta = jnp.where(lane == 0, i1.astype(F32), 0.0)
    meta = jnp.where(lane == 1, i2.astype(F32), meta)
    meta = jnp.where(lane == 2, rank1, meta)
    meta = jnp.where(lane == 3, rank2, meta)
    meta = jnp.where(lane == 4, w1, meta)
    meta = jnp.where(lane == 5, w2, meta)
    meta_ref[...] = meta
    new_carry = carry + jnp.sum(onehot.astype(F32), axis=0, keepdims=True)
    carry_ref[...] = new_carry
    total_ref[...] = new_carry


def _router(h2, g, router_w):
    t, d = h2.shape
    tc = TOK_CHUNK
    nc = t // tc
    wr = jnp.zeros((d, LANES), F32).at[:, :router_w.shape[1]].set(router_w)
    wr_hi = wr.astype(BF16)
    wr = jnp.stack([wr_hi, (wr - wr_hi.astype(F32)).astype(BF16)])
    last = nc - 1
    return pl.pallas_call(
        _router_kernel,
        grid=(nc + 1,),
        in_specs=[
            pl.BlockSpec((tc, d), lambda i: (jnp.minimum(i, last), 0)),
            _const_spec((1, d)),
            _const_spec((2, d, LANES)),
        ],
        out_specs=[
            pl.BlockSpec((tc * SUBLANES, LANES), lambda i: (i, 0)),
            pl.BlockSpec((tc, LANES), lambda i: (jnp.minimum(i, last), 0)),
            pl.BlockSpec((1, LANES), lambda i: (0, 0)),
        ],
        out_shape=[
            jax.ShapeDtypeStruct(((t + tc) * SUBLANES, LANES), F32),
            jax.ShapeDtypeStruct((t, LANES), F32),
            jax.ShapeDtypeStruct((1, LANES), F32),
        ],
        scratch_shapes=[pltpu.VMEM((1, LANES), F32)],
        compiler_params=_cparams("arbitrary"),
        name="moe_router",
    )(h2, g.reshape(1, d), wr)


def _row_copy(src, dst, sem):
    return pltpu.make_async_copy(src, dst, sem)


def _experts_kernel(te_ref, nt_ref, nreal_ref, inv_cur_ref, inv_nxt_ref, inv_nx2_ref, inv_prv_ref,
                    hn_hbm, wg_ref, wu_ref, wd_ref, y_hbm, xin_ref, x_ref, acc_ref, yout_ref,
                    gsem, ssem, *, n_tokens):
    q = pl.program_id(0)
    f = pl.program_id(1)
    nf = pl.num_programs(1)
    nt = nt_ref[0]
    rows = x_ref.shape[0]
    live = q < nt
    slot = q % 2
    other = 1 - slot
    xslot = q % 3
    xahead = (q + 2) % 3
    share = -(-rows // MOE_STEPS)
    last_tile = nreal_ref.shape[0] - 1
    n_prev = jnp.where(q >= 1, nreal_ref[jnp.clip(q - 1, 0, last_tile)], 0)
    n_prev2 = nreal_ref[jnp.clip(q - 2, 0, last_tile)]

    def token_row(a):
        return jnp.where(a >= n_tokens, a - n_tokens, a)

    def gather_row(src, dst_slot, j):
        _row_copy(_row_tile(hn_hbm, src), _row_tile(xin_ref.at[dst_slot], j), gsem.at[dst_slot]).start()

    def scatter_row(src_slot, j, dst):
        _row_copy(_row_tile(yout_ref.at[src_slot], j), _row_tile(y_hbm, dst), ssem.at[src_slot]).start()

    def wait_gathered(s):
        _row_copy(hn_hbm.at[pl.ds(0, rows * SUBLANES), :], xin_ref.at[s], gsem.at[s]).wait()

    def wait_scattered(s, n):
        size = pl.multiple_of(n * SUBLANES, SUBLANES)
        _row_copy(yout_ref.at[s, pl.ds(0, size), :], y_hbm.at[pl.ds(0, size), :], ssem.at[s]).wait()

    @pl.when((q == 0) & (f == 0))
    def _():
        def one(j, c):
            gather_row(token_row(inv_cur_ref[0, 0, j]), 0, j)
            gather_row(token_row(inv_nxt_ref[0, 0, j]), 1, j)
            return c
        lax.fori_loop(0, rows, one, 0, unroll=ROW_BATCH)

    @pl.when((q >= 2) & (q <= nt) & (f == 0))
    def _():
        wait_scattered(slot, n_prev2)

    @pl.when((q <= nt + 1) & (f == 0))
    def _():
        wait_gathered(xslot)

    @pl.when(live & (f == 0))
    def _():
        x_ref[...] = _load_row_tiles(xin_ref.at[xslot]).astype(BF16)
        acc_ref[...] = jnp.zeros_like(acc_ref)

    @pl.when(live)
    def _():
        lo = f * share

        def issue_batch(b):
            base = lo + b * ROW_BATCH
            srcs = [token_row(inv_nx2_ref[0, 0, base + u]) for u in range(ROW_BATCH)]
            dsts = [inv_prv_ref[0, 0, base + u] for u in range(ROW_BATCH)]
            for u in range(ROW_BATCH):
                gather_row(srcs[u], xahead, base + u)
            for u in range(ROW_BATCH):
                @pl.when(base + u < n_prev)
                def _(u=u):
                    scatter_row(other, base + u, dsts[u])

        n_sub = wd_ref.shape[1] // FFN_CHUNK
        n_batches = share // ROW_BATCH
        n_issue = max(1, n_sub - 3)
        x = x_ref[...]
        acc = acc_ref[...]
        for c in range(n_sub):
            if c < n_issue:
                for b in range(c * n_batches // n_issue, (c + 1) * n_batches // n_issue):
                    issue_batch(b)
            cols = slice(c * FFN_CHUNK, (c + 1) * FFN_CHUNK)
            gate = _dot(x, wg_ref[0, :, cols])
            up = _dot(x, wu_ref[0, :, cols])
            act = (gate * jax.nn.sigmoid(gate) * up).astype(BF16)
            acc = acc + _dot(act, wd_ref[0, cols, :])
        acc_ref[...] = acc

    @pl.when(live & (f == nf - 1))
    def _():
        _store_row_tiles(yout_ref.at[slot], acc_ref[...])

    @pl.when((q == nt) & (f == 0))
    def _():
        def one(j, c):
            scatter_row(other, j, inv_prv_ref[0, 0, j])
            return c
        lax.fori_loop(0, n_prev, one, 0)
        wait_scattered(other, n_prev)


def _experts(hn_rows, inv, tile_expert, n_tiles, tile_real, w_gate, w_up, w_down, n_tokens):
    d = SUBLANES * LANES
    ne, ff, _ = w_down.shape
    assert w_down.shape[2] == d
    nf = ff // MOE_CHUNK
    assert nf == MOE_STEPS and MOE_ROWS % (MOE_STEPS * ROW_BATCH) == 0
    nq = inv.shape[0]

    def tile(q, nt):
        return jnp.minimum(q, nt[0] - 1)

    def chunk(q, f, nt):
        return jnp.where(q < nt[0], f, nf - 1)

    idx_block = (1, 1, MOE_ROWS)
    return pl.pallas_call(
        functools.partial(_experts_kernel, n_tokens=n_tokens),
        grid_spec=pltpu.PrefetchScalarGridSpec(
            num_scalar_prefetch=3,
            grid=(nq + 2, nf),
            in_specs=[
                pl.BlockSpec(idx_block, lambda q, f, te, nt, nr: (jnp.minimum(q, nq - 1), 0, 0),
                             memory_space=pltpu.SMEM),
                pl.BlockSpec(idx_block, lambda q, f, te, nt, nr: (jnp.minimum(q + 1, nq - 1), 0, 0),
                             memory_space=pltpu.SMEM),
                pl.BlockSpec(idx_block, lambda q, f, te, nt, nr: (jnp.minimum(q + 2, nq - 1), 0, 0),
                             memory_space=pltpu.SMEM),
                pl.BlockSpec(idx_block, lambda q, f, te, nt, nr: (jnp.clip(q - 1, 0, nq - 1), 0, 0),
                             memory_space=pltpu.SMEM),
                pl.BlockSpec(memory_space=pl.ANY),
                pl.BlockSpec((1, d, MOE_CHUNK),
                             lambda q, f, te, nt, nr: (te[tile(q, nt)], 0, chunk(q, f, nt))),
                pl.BlockSpec((1, d, MOE_CHUNK),
                             lambda q, f, te, nt, nr: (te[tile(q, nt)], 0, chunk(q, f, nt))),
                pl.BlockSpec((1, MOE_CHUNK, d),
                             lambda q, f, te, nt, nr: (te[tile(q, nt)], chunk(q, f, nt), 0)),
            ],
            out_specs=pl.BlockSpec(memory_space=pl.ANY),
            scratch_shapes=[
                pltpu.VMEM((3, MOE_ROWS * SUBLANES, LANES), F32),
                pltpu.VMEM((MOE_ROWS, d), BF16),
                pltpu.VMEM((MOE_ROWS, d), F32),
                pltpu.VMEM((2, MOE_ROWS * SUBLANES, LANES), F32),
                pltpu.SemaphoreType.DMA((3,)),
                pltpu.SemaphoreType.DMA((2,)),
            ],
        ),
        out_shape=jax.ShapeDtypeStruct((2 * n_tokens * SUBLANES, LANES), F32),
        compiler_params=pltpu.CompilerParams(
            dimension_semantics=("arbitrary", "arbitrary"), vmem_limit_bytes=VMEM_LIMIT,
            disable_bounds_checks=True),
        name="moe_experts",
    )(tile_expert, n_tiles, tile_real, inv, inv, inv, inv, hn_rows, w_gate, w_up, w_down)


def _invert_kernel(dest_ref, unset_hbm, inv_ref, sem):
    fill = pltpu.make_async_copy(unset_hbm, inv_ref, sem)
    fill.start()
    fill.wait()
    batch = 2 * ROW_BATCH

    def put(b, c):
        a0 = b * batch
        rows = [dest_ref[a0 + u] for u in range(batch)]
        for u in range(batch):
            inv_ref[rows[u]] = a0 + u
        return c

    lax.fori_loop(0, dest_ref.shape[0] // batch, put, 0)


def _invert(dest, n_rows):
    assert dest.shape[0] % (2 * ROW_BATCH) == 0
    return pl.pallas_call(
        _invert_kernel,
        in_specs=[pl.BlockSpec(memory_space=pltpu.SMEM), pl.BlockSpec(memory_space=pl.ANY)],
        out_specs=pl.BlockSpec(memory_space=pltpu.SMEM),
        out_shape=jax.ShapeDtypeStruct((n_rows,), jnp.int32),
        scratch_shapes=[pltpu.SemaphoreType.DMA(())],
        name="moe_invert",
    )(dest, jnp.full((n_rows,), dest.shape[0], jnp.int32))


def _combine_kernel(h_ref, y1_ref, y2_ref, wts_ref, g_ref, o_ref):
    wt = wts_ref[...]
    moe = wt[:, 0:1] * _load_row_tiles(y1_ref) + wt[:, 1:2] * _load_row_tiles(y2_ref)
    o_ref[...] = _rms(h_ref[...] + moe, g_ref[...])


def _combine(h2, y, wts, g_final):
    t, d = h2.shape
    ts = SEQ_TILE
    nt = t // ts
    return pl.pallas_call(
        _combine_kernel,
        grid=(nt,),
        in_specs=[
            pl.BlockSpec((ts, d), lambda i: (i, 0)),
            pl.BlockSpec((ts * SUBLANES, LANES), lambda i: (i, 0)),
            pl.BlockSpec((ts * SUBLANES, LANES), lambda i: (nt + i, 0)),
            pl.BlockSpec((ts, 2), lambda i: (i, 0)),
            _const_spec((1, d)),
        ],
        out_specs=pl.BlockSpec((ts, d), lambda i: (i, 0)),
        out_shape=jax.ShapeDtypeStruct((t, d), F32),
        compiler_params=_cparams("parallel"),
        name="moe_combine",
    )(h2, y, y, wts, g_final.reshape(1, d))


def _moe_block(h2, g, router_w, w_gate, w_up, w_down, g_final):
    t, d = h2.shape
    ne = w_down.shape[0]
    nq = (2 * t) // MOE_ROWS + ne
    n_rows = nq * MOE_ROWS

    hn_rows, meta, total = _router(h2, g, router_w)

    experts = meta[:, 0:2].astype(jnp.int32)
    ranks = meta[:, 2:4].astype(jnp.int32)
    wts = meta[:, 4:6]
    counts = total[0, :ne].astype(jnp.int32)
    tiles_per = (counts + MOE_ROWS - 1) // MOE_ROWS
    tile_end = jnp.cumsum(tiles_per)
    tile_start = tile_end - tiles_per
    n_tiles = tile_end[-1]
    dest = (jnp.take(tile_start * MOE_ROWS, experts) + ranks).T.reshape(-1)
    qs = jnp.arange(nq, dtype=jnp.int32)
    tile_expert = jnp.minimum(jnp.sum((qs[:, None] >= tile_end[None, :]).astype(jnp.int32), axis=1),
                              ne - 1)
    tile_real = jnp.clip(jnp.take(counts, tile_expert) - (qs - jnp.take(tile_start, tile_expert)) * MOE_ROWS,
                         0, MOE_ROWS)
    inv = _invert(dest, n_rows)
    y = _experts(hn_rows, inv.reshape(nq, 1, MOE_ROWS), tile_expert, n_tiles.reshape(1), tile_real,
                 w_gate, w_up, w_down, t)
    return _combine(h2, y, wts, g_final)


def kernel(x, mem, ev_norm_mix, ev_w_in, ev_conv_a, ev_pool_w, ev_pool_scale, ev_w_out, ev_norm_ffn, ev_ffn_gu, ev_ffn_down, od_norm_mix, od_pw1_w, od_pw1_b, od_dw_w, od_dw_b, od_ln_g, od_ln_b, od_pw2_w, od_pw2_b, od_norm_moe, od_router, od_moe_gu, od_moe_down, xa_norm, xa_norm_mem, xa_wq, xa_wkv, xa_wo, final_norm):
    b, s, d = x.shape
    assert ev_w_in.shape[0] == 1 and od_pw1_w.shape[0] == 1 and xa_wq.shape[0] == 2
    kv = _memory_kv(mem, xa_norm_mem, xa_wkv)
    n_steps = (b * s) // SEQ_TILE
    h, (moe_down, ffn_gu, ffn_down, wq, wo, pw1_w, pw2_w) = _even_mixer(
        x, ev_norm_mix[0], ev_w_in[0], ev_conv_a[0], ev_pool_w[0], ev_pool_scale[0], ev_w_out[0],
        [_CastRider(od_moe_down[0], n_steps), _CastRider(ev_ffn_gu[0], n_steps),
         _CastRider(ev_ffn_down[0], n_steps, repeat=2), _CastRider(xa_wq, n_steps),
         _CastRider(xa_wo, n_steps), _CastRider(od_pw1_w[0], n_steps),
         _CastRider(od_pw2_w[0], n_steps)])
    h = _cross_attn(h, xa_norm[0], wq[0], kv[0], wo[0])
    h, moe_gate = _dense_swiglu(h.reshape(b * s, d), ev_norm_ffn[0], ffn_gu[0], ffn_down[0],
                                _CastRider(od_moe_gu[0], n_steps, col_blocks=2, col_block=0))
    h, moe_up = _conformer(h.reshape(b, s, d), od_norm_mix[0], pw1_w[0], od_pw1_b[0], od_dw_w[0],
                           od_dw_b[0], od_ln_g[0], od_ln_b[0], pw2_w[0], od_pw2_b[0],
                           _CastRider(od_moe_gu[0], n_steps, col_blocks=2, col_block=1))
    h = _cross_attn(h, xa_norm[1], wq[1], kv[1], wo[1])
    out = _moe_block(h.reshape(b * s, d), od_norm_moe[0], od_router[0], moe_gate, moe_up,
                     moe_down, final_norm)
    return out.reshape(b, s, d)
```
